```python
import math
import jax, jax.numpy as jnp
from jax import lax
import numpy as np

D_MODEL = 1024
BATCH = 16
SEQ = 2048
DEPTH = 2
DEC_BATCH = 8
DEC_SEQ = 32
PAST_LEN = 2048

CHUNK = 64
Q_BLOCK = 128
A_HEADS = 8
A_KV_HEADS = 2
A_HEAD_DIM = 64
IDX_HEADS = 8
IDX_DIM = 64
TOPK_MAX = 256
B_HEADS = 4
B_KEY_DIM = 128
B_VAL_DIM = 128
CONV_WIDTH = 4
D_FF = 2816
N_EXPERTS = 8
TOP_K = 2
D_FF_EXPERT = 3584
LN_EPS = 1e-5
RMS_EPS = 1e-6
DEEPNORM_ALPHA = (2 * DEPTH) ** 0.25
DEEPNORM_BETA = (8 * DEPTH) ** -0.25

A_Q = A_HEADS * A_HEAD_DIM
A_KV = A_KV_HEADS * A_HEAD_DIM
B_QK = B_HEADS * B_KEY_DIM
B_V = B_HEADS * B_VAL_DIM
B_CONV_CH = 2 * B_QK + B_V
SPLIT_WIDTHS = (A_Q, A_KV, A_KV, IDX_HEADS * IDX_DIM, IDX_DIM, IDX_HEADS,
                B_CONV_CH, B_HEADS, B_HEADS, B_V, 2 * D_MODEL)
SPLIT_POINTS = tuple(int(s) for s in np.cumsum(SPLIT_WIDTHS)[:-1])
D_IN = int(sum(SPLIT_WIDTHS))
N_DENSE = (DEPTH + 1) // 2
N_MOE = DEPTH // 2
ALIBI_SLOPES = tuple(2.0 ** (-8.0 * (h + 1) / A_HEADS) for h in range(A_HEADS))

kernel_name = 'hybrid_dsa_gdn_streaming_step'


def layer_norm(x, g, b):
    xf = x.astype(jnp.float32)
    mu = jnp.mean(xf, axis=-1, keepdims=True)
    var = jnp.mean(jnp.square(xf - mu), axis=-1, keepdims=True)
    return ((xf - mu) * lax.rsqrt(var + LN_EPS) * g + b).astype(x.dtype)


def l2norm(x):
    xf = x.astype(jnp.float32)
    return xf * lax.rsqrt(jnp.sum(xf * xf, axis=-1, keepdims=True) + RMS_EPS)


def causal_conv(u, buf, w):
    T = u.shape[1]
    up = jnp.concatenate([buf.astype(u.dtype), u], axis=1)
    y = up[:, 0:T] * w[0]
    for j in range(1, CONV_WIDTH):
        y = y + up[:, j:j + T] * w[j]
    return jax.nn.silu(y), up[:, up.shape[1] - (CONV_WIDTH - 1):]


def dsa_attention(q, qi, wi, k, v, ik, q_start):
    bsz, T = q.shape[0], q.shape[1]
    S = k.shape[1]
    n_sel = min(TOPK_MAX, S // 4)
    qb_len = Q_BLOCK if T % Q_BLOCK == 0 else T
    nb = T // qb_len
    grp = A_HEADS // A_KV_HEADS
    slopes = jnp.asarray(ALIBI_SLOPES, jnp.float32).reshape(A_KV_HEADS, grp)
    k_pos = jnp.arange(S)
    q_pos = q_start + jnp.arange(T)

    def blocks(a):
        return jnp.moveaxis(a.reshape((bsz, nb, qb_len) + a.shape[2:]), 1, 0)

    def attend_block(xs):
        q_b, qi_b, wi_b, t_pos = xs
        idx = jnp.einsum('bqhd,bsd->bqhs', qi_b, ik).astype(jnp.float32)
        score = jnp.einsum('bqhs,bqh->bqs', jax.nn.relu(idx), wi_b.astype(jnp.float32)) * IDX_DIM ** -0.5
        admissible = (k_pos[None, :] // CHUNK) <= (t_pos[:, None] // CHUNK)
        score = jnp.where(admissible[None], score, -jnp.inf)
        top_val, top_idx = lax.top_k(score, n_sel)
        valid = jnp.isfinite(top_val)
        k_g = jax.vmap(lambda kk, ii: kk[ii])(k, top_idx)
        v_g = jax.vmap(lambda vv, ii: vv[ii])(v, top_idx)
        q_g = q_b.reshape(bsz, qb_len, A_KV_HEADS, grp, A_HEAD_DIM)
        logits = jnp.einsum('bqgrd,bqkgd->bqgrk', q_g, k_g).astype(jnp.float32) * A_HEAD_DIM ** -0.5
        dist = jnp.abs(t_pos[None, :, None] - top_idx).astype(jnp.float32)
        logits = logits - slopes[None, None, :, :, None] * dist[:, :, None, None, :]
        logits = jnp.where(valid[:, :, None, None, :], logits, -jnp.inf)
        p = jax.nn.softmax(logits, axis=-1).astype(v.dtype)
        o = jnp.einsum('bqgrk,bqkgd->bqgrd', p, v_g)
        return o.reshape(bsz, qb_len, A_Q)

    out = lax.map(attend_block, (blocks(q), blocks(qi), blocks(wi), q_pos.reshape(nb, qb_len)))
    return jnp.moveaxis(out, 0, 1).reshape(bsz, T, A_Q)


def gated_delta_rule(q, k, v, g, beta, s0):
    bsz, T, H, dk = q.shape
    dv = v.shape[-1]
    c = CHUNK if T % CHUNK == 0 else T
    n = T // c

    def to_chunks(a):
        a = a.reshape((bsz, n, c, H) + a.shape[3:])
        return a.transpose((1, 0, 3, 2) + tuple(range(4, a.ndim)))

    q = to_chunks(q * dk ** -0.5)
    k = to_chunks(k)
    v = to_chunks(v)
    g = to_chunks(g)
    beta = to_chunks(beta)
    G = jnp.cumsum(g, axis=-1)
    incl = jnp.tril(jnp.ones((c, c), dtype=bool))
    strict = jnp.tril(jnp.ones((c, c), dtype=bool), -1)
    diff = G[..., :, None] - G[..., None, :]
    decay = jnp.where(incl, jnp.exp(jnp.where(incl, diff, 0.0)), 0.0)
    k_beta = k * beta[..., None]
    m = jnp.where(strict, jnp.einsum('nbhik,nbhjk->nbhij', k_beta, k) * decay, 0.0)
    tri = m + jnp.eye(c, dtype=jnp.float32)
    rhs = jnp.concatenate([v * beta[..., None], k_beta * jnp.exp(G)[..., None]], axis=-1)
    sol = lax.linalg.triangular_solve(tri, rhs, left_side=True, lower=True, unit_diagonal=True)
    u, w = sol[..., :dv], sol[..., dv:]
    qk = jnp.where(incl, jnp.einsum('nbhik,nbhjk->nbhij', q, k) * decay, 0.0)
    q_dec = q * jnp.exp(G)[..., None]
    k_tail = k * jnp.exp(G[..., -1:] - G)[..., None]

    def step(s, xs):
        u_c, w_c, qk_c, qd_c, kt_c, gl_c = xs
        v_new = u_c - jnp.einsum('bhck,bhkv->bhcv', w_c, s)
        o = jnp.einsum('bhck,bhkv->bhcv', qd_c, s) + jnp.einsum('bhij,bhjv->bhiv', qk_c, v_new)
        s = s * jnp.exp(gl_c)[..., None, None] + jnp.einsum('bhck,bhcv->bhkv', kt_c, v_new)
        return s, o

    s_fin, o = lax.scan(step, s0, (u, w, qk, q_dec, k_tail, G[..., -1]))
    o = o.transpose(1, 0, 3, 2, 4).reshape(bsz, T, H, dv)
    return o, s_fin


def token_mixer(h, past_k, past_v, past_ik, conv_buf, s0, w_in, conv_w, a_log, dt_bias,
                gnorm_g, w_oa, w_ob, w_out):
    bsz, T, _ = h.shape
    P = past_k.shape[1]
    qa, ka, va, qi, ki, wi, qkv_b, a_b, b_b, z_b, gates = jnp.split(h @ w_in, SPLIT_POINTS, axis=-1)
    qa = qa.reshape(bsz, T, A_HEADS, A_HEAD_DIM)
    ka = ka.reshape(bsz, T, A_KV_HEADS, A_HEAD_DIM)
    va = va.reshape(bsz, T, A_KV_HEADS, A_HEAD_DIM)
    qi = qi.reshape(bsz, T, IDX_HEADS, IDX_DIM)
    k_all = jnp.concatenate([past_k.astype(ka.dtype), ka], axis=1)
    v_all = jnp.concatenate([past_v.astype(va.dtype), va], axis=1)
    ik_all = jnp.concatenate([past_ik.astype(ki.dtype), ki], axis=1)
    o_a = dsa_attention(qa, qi, wi, k_all, v_all, ik_all, P)
    c, conv_new = causal_conv(qkv_b, conv_buf, conv_w)
    qb, kb, vb = jnp.split(c, (B_QK, 2 * B_QK), axis=-1)
    qb = l2norm(qb.reshape(bsz, T, B_HEADS, B_KEY_DIM))
    kb = l2norm(kb.reshape(bsz, T, B_HEADS, B_KEY_DIM))
    vb = vb.reshape(bsz, T, B_HEADS, B_VAL_DIM).astype(jnp.float32)
    g = -jnp.exp(a_log.astype(jnp.float32)) * jax.nn.softplus(a_b.astype(jnp.float32) + dt_bias.astype(jnp.float32))
    beta = jax.nn.sigmoid(b_b.astype(jnp.float32))
    o_b, s_new = gated_delta_rule(qb, kb, vb, g, beta, s0.astype(jnp.float32))
    o_b = o_b * lax.rsqrt(jnp.mean(o_b * o_b, axis=-1, keepdims=True) + RMS_EPS) * gnorm_g.astype(jnp.float32)
    o_b = o_b * jax.nn.silu(z_b.astype(jnp.float32).reshape(bsz, T, B_HEADS, B_VAL_DIM))
    o_b = o_b.reshape(bsz, T, B_V).astype(h.dtype)
    g_a, g_b = jnp.split(jax.nn.sigmoid(gates), 2, axis=-1)
    y = (g_a * (o_a @ w_oa) + g_b * (o_b @ w_ob)) @ w_out
    return y, ka, va, ki, conv_new, s_new


def swiglu(h, w1, w3, w2):
    return (jax.nn.silu(h @ w1) * (h @ w3)) @ w2


def moe_ffn(h, router, w1, w3, w2):
    logits = (h @ router).astype(jnp.float32)
    top_val, top_idx = lax.top_k(logits, TOP_K)
    top_w = jax.nn.softmax(top_val, axis=-1)
    gate = jnp.sum(jax.nn.one_hot(top_idx, N_EXPERTS, dtype=jnp.float32) * top_w[..., None], axis=-2)
    out = gate[..., 0:1].astype(h.dtype) * swiglu(h, w1[0], w3[0], w2[0])
    for e in range(1, N_EXPERTS):
        out = out + gate[..., e:e + 1].astype(h.dtype) * swiglu(h, w1[e], w3[e], w2[e])
    return out


def trunk(x, past_k, past_v, past_ik, conv0, s0, weights):
    (ln_in_g, ln_in_b, w_in, conv_w, a_log, dt_bias, gnorm_g, w_oa, w_ob, w_out,
     ln1_g, ln1_b, ln2_g, ln2_b, ffn_w1, ffn_w3, ffn_w2, moe_router, moe_w1, moe_w3, moe_w2) = weights
    h = layer_norm(x, ln_in_g, ln_in_b)
    ks, vs, iks, convs, ss = [], [], [], [], []
    for l in range(DEPTH):
        y, k_new, v_new, ik_new, conv_new, s_new = token_mixer(
            h, past_k[l], past_v[l], past_ik[l], conv0[l], s0[l], w_in[l], conv_w[l], a_log[l],
            dt_bias[l], gnorm_g[l], w_oa[l], w_ob[l], w_out[l])
        h = layer_norm(DEEPNORM_ALPHA * h + y, ln1_g[l], ln1_b[l])
        if l % 2 == 0:
            f = swiglu(h, ffn_w1[l // 2], ffn_w3[l // 2], ffn_w2[l // 2])
        else:
            f = moe_ffn(h, moe_router[l // 2], moe_w1[l // 2], moe_w3[l // 2], moe_w2[l // 2])
        h = layer_norm(DEEPNORM_ALPHA * h + f, ln2_g[l], ln2_b[l])
        ks.append(k_new)
        vs.append(v_new)
        iks.append(ik_new)
        convs.append(conv_new)
        ss.append(s_new)
    return h, jnp.stack(ks), jnp.stack(vs), jnp.stack(iks), jnp.stack(convs), jnp.stack(ss)


def setup_inputs(seed: int = 0) -> dict:
    key = jax.random.key(seed)
    ks = jax.random.split(key, 32)
    f32 = jnp.float32
    nrm = lambda k, shape, s: jax.random.normal(k, shape, f32) * s
    dt = jnp.exp(jax.random.uniform(ks[10], (DEPTH, B_HEADS), f32) * (math.log(0.1) - math.log(0.001)) + math.log(0.001))
    return {
        'x_prompt': nrm(ks[0], (BATCH, SEQ, D_MODEL), 1.0),
        'x_sample': nrm(ks[1], (DEC_BATCH, DEC_SEQ, D_MODEL), 1.0),
        'cache_k': nrm(ks[2], (DEPTH, DEC_BATCH, PAST_LEN, A_KV_HEADS, A_HEAD_DIM), 1.0),
        'cache_v': nrm(ks[3], (DEPTH, DEC_BATCH, PAST_LEN, A_KV_HEADS, A_HEAD_DIM), 1.0),
        'cache_idx_k': nrm(ks[4], (DEPTH, DEC_BATCH, PAST_LEN, IDX_DIM), 1.0),
        'state_conv': nrm(ks[5], (DEPTH, DEC_BATCH, CONV_WIDTH - 1, B_CONV_CH), 1.0),
        'state_gdn': nrm(ks[6], (DEPTH, DEC_BATCH, B_HEADS, B_KEY_DIM, B_VAL_DIM), 0.1),
        'ln_in_g': 1.0 + nrm(ks[7], (D_MODEL,), 0.02),
        'ln_in_b': nrm(ks[8], (D_MODEL,), 0.02),
        'w_in': nrm(ks[9], (DEPTH, D_MODEL, D_IN), D_MODEL ** -0.5),
        'conv_w': nrm(ks[11], (DEPTH, CONV_WIDTH, B_CONV_CH), CONV_WIDTH ** -0.5),
        'a_log': jnp.log(jax.random.uniform(ks[12], (DEPTH, B_HEADS), f32, 1.0, 16.0)),
        'dt_bias': dt + jnp.log(-jnp.expm1(-dt)),
        'gnorm_g': 1.0 + nrm(ks[13], (DEPTH, B_VAL_DIM), 0.02),
        'w_oa': nrm(ks[14], (DEPTH, A_Q, D_MODEL), A_Q ** -0.5 * DEEPNORM_BETA),
        'w_ob': nrm(ks[15], (DEPTH, B_V, D_MODEL), B_V ** -0.5 * DEEPNORM_BETA),
        'w_out': nrm(ks[16], (DEPTH, D_MODEL, D_MODEL), D_MODEL ** -0.5 * DEEPNORM_BETA),
        'ln1_g': 1.0 + nrm(ks[17], (DEPTH, D_MODEL), 0.02),
        'ln1_b': nrm(ks[18], (DEPTH, D_MODEL), 0.02),
        'ln2_g': 1.0 + nrm(ks[19], (DEPTH, D_MODEL), 0.02),
        'ln2_b': nrm(ks[20], (DEPTH, D_MODEL), 0.02),
        'ffn_w1': nrm(ks[21], (N_DENSE, D_MODEL, D_FF), D_MODEL ** -0.5),
        'ffn_w3': nrm(ks[22], (N_DENSE, D_MODEL, D_FF), D_MODEL ** -0.5),
        'ffn_w2': nrm(ks[23], (N_DENSE, D_FF, D_MODEL), D_FF ** -0.5 * DEEPNORM_BETA),
        'moe_router': nrm(ks[24], (N_MOE, D_MODEL, N_EXPERTS), D_MODEL ** -0.5),
        'moe_w1': nrm(ks[25], (N_MOE, N_EXPERTS, D_MODEL, D_FF_EXPERT), D_MODEL ** -0.5),
        'moe_w3': nrm(ks[26], (N_MOE, N_EXPERTS, D_MODEL, D_FF_EXPERT), D_MODEL ** -0.5),
        'moe_w2': nrm(ks[27], (N_MOE, N_EXPERTS, D_FF_EXPERT, D_MODEL), D_FF_EXPERT ** -0.5 * DEEPNORM_BETA),
    }


def reference(x_prompt, x_sample, cache_k, cache_v, cache_idx_k, state_conv, state_gdn,
              ln_in_g, ln_in_b, w_in, conv_w, a_log, dt_bias, gnorm_g, w_oa, w_ob, w_out,
              ln1_g, ln1_b, ln2_g, ln2_b, ffn_w1, ffn_w3, ffn_w2,
              moe_router, moe_w1, moe_w3, moe_w2):
    weights = (ln_in_g, ln_in_b, w_in, conv_w, a_log, dt_bias, gnorm_g, w_oa, w_ob, w_out,
               ln1_g, ln1_b, ln2_g, ln2_b, ffn_w1, ffn_w3, ffn_w2, moe_router, moe_w1, moe_w3, moe_w2)
    bp = x_prompt.shape[0]
    empty_kv = jnp.zeros((DEPTH, bp, 0, A_KV_HEADS, A_HEAD_DIM), x_prompt.dtype)
    empty_ik = jnp.zeros((DEPTH, bp, 0, IDX_DIM), x_prompt.dtype)
    conv0 = jnp.zeros((DEPTH, bp, CONV_WIDTH - 1, B_CONV_CH), x_prompt.dtype)
    s0 = jnp.zeros((DEPTH, bp, B_HEADS, B_KEY_DIM, B_VAL_DIM), jnp.float32)
    y_prompt, pk, pv, pik, pconv, ps = trunk(x_prompt, empty_kv, empty_kv, empty_ik, conv0, s0, weights)
    y_sample, sk, sv, sik, sconv, ss = trunk(x_sample, cache_k, cache_v, cache_idx_k, state_conv, state_gdn, weights)
    return (y_prompt, y_sample, pk, pv, pik, pconv, ps, sk, sv, sik, sconv, ss)
```

```python
import functools
import math

import jax
import jax.numpy as jnp
from jax import lax
from jax.experimental import pallas as pl
from jax.experimental.pallas import tpu as pltpu

F32 = jnp.float32
BF16 = jnp.bfloat16
I32 = jnp.int32

D_MODEL = 1024
DEPTH = 2
CHUNK = 64
Q_BLOCK = 128
A_HEADS = 8
A_KV_HEADS = 2
A_HEAD_DIM = 64
IDX_HEADS = 8
IDX_DIM = 64
TOPK_MAX = 256
B_HEADS = 4
B_KEY_DIM = 128
B_VAL_DIM = 128
CONV_WIDTH = 4
D_FF = 2816
N_EXPERTS = 8
D_FF_EXPERT = 3584
LN_EPS = 1e-5
RMS_EPS = 1e-6
DEEPNORM_ALPHA = (2 * DEPTH) ** 0.25
ALIBI_SLOPES = tuple(2.0 ** (-8.0 * (h + 1) / A_HEADS) for h in range(A_HEADS))

A_Q = A_HEADS * A_HEAD_DIM
A_KV = A_KV_HEADS * A_HEAD_DIM
B_QK = B_HEADS * B_KEY_DIM
B_V = B_HEADS * B_VAL_DIM
B_CONV_CH = 2 * B_QK + B_V

COL_GA = 0
COL_GB = 1024
COL_QB = 2048
COL_KB = 2560
COL_VB = 3072
COL_Z = 3584
COL_QA = 4096
COL_QI = 4608
COL_KV = 5120
COL_MISC = 5376
PROJ_W = 5632
MISC_KI = 0
MISC_WI = 64
MISC_A = 72
MISC_B = 76

INT_MIN = -(2 ** 31)
VMEM_LIMIT = 56 * 1024 * 1024


def _pick_tile(n, candidates):
    for c in candidates:
        if n % c == 0:
            return c
    raise ValueError(f"no tile for {n}")


def _params(sem):
    return pltpu.CompilerParams(dimension_semantics=sem, vmem_limit_bytes=VMEM_LIMIT)


def _ln(x, g, b):
    mu = jnp.mean(x, axis=-1, keepdims=True)
    xc = x - mu
    var = jnp.mean(xc * xc, axis=-1, keepdims=True)
    return xc * lax.rsqrt(var + LN_EPS) * g + b


def _dot(a, b):
    return jnp.dot(a, b, preferred_element_type=F32)


def _dot_nt(a, b):
    return lax.dot_general(a, b, (((1,), (1,)), ((), ())), preferred_element_type=F32)


def _mm1(a, b):
    return _dot(a.astype(BF16), b.astype(BF16))


def _mm1_nt(a, b):
    return _dot_nt(a.astype(BF16), b.astype(BF16))


def _mm1_tn(a, b):
    c = a.shape[0]
    if c < 128:
        a = jnp.concatenate([a, jnp.zeros((128 - c, a.shape[1]), a.dtype)], axis=0)
        b = jnp.concatenate([b, jnp.zeros((128 - c, b.shape[1]), b.dtype)], axis=0)
    return _dot(a.T.astype(BF16), b.astype(BF16))


def _split2(x):
    hi = x.astype(BF16)
    lo = (x - hi.astype(F32)).astype(BF16)
    return hi, lo


def _mm3(a, b):
    ah, al = _split2(a)
    bh, bl = _split2(b)
    return _dot(ah, bh) + (_dot(ah, bl) + _dot(al, bh))


def _silu(x):
    return x * (1.0 / (1.0 + jnp.exp(-x)))


def _sigmoid(x):
    return 1.0 / (1.0 + jnp.exp(-x))


def _ln_kernel(x_ref, g_ref, b_ref, o_ref):
    o_ref[...] = _ln(x_ref[...], g_ref[...], b_ref[...])


def _layer_norm_rows(x, g, b):
    n = x.shape[0]
    tm = _pick_tile(n, (768, 512, 256, 128, 64, 32, 16, 8))
    return pl.pallas_call(
        _ln_kernel,
        grid=(n // tm,),
        in_specs=[pl.BlockSpec((tm, D_MODEL), lambda i: (i, 0)),
                  pl.BlockSpec((1, D_MODEL), lambda i: (0, 0)),
                  pl.BlockSpec((1, D_MODEL), lambda i: (0, 0))],
        out_specs=pl.BlockSpec((tm, D_MODEL), lambda i: (i, 0)),
        out_shape=jax.ShapeDtypeStruct((n, D_MODEL), F32),
        compiler_params=_params(("parallel",)),
        name="ln_in",
    )(x, g.reshape(1, D_MODEL), b.reshape(1, D_MODEL))


def _proj_kernel(h_ref, w_ref, o_ref):
    o_ref[...] = _dot(h_ref[...].astype(BF16), w_ref[...])


def _project(h, w_packed):
    n = h.shape[0]
    tm = _pick_tile(n, (768, 512, 256, 128, 64, 32, 16, 8))
    tn = 1408
    return pl.pallas_call(
        _proj_kernel,
        grid=(n // tm, PROJ_W // tn),
        in_specs=[pl.BlockSpec((tm, D_MODEL), lambda i, j: (i, 0)),
                  pl.BlockSpec((D_MODEL, tn), lambda i, j: (0, j))],
        out_specs=pl.BlockSpec((tm, tn), lambda i, j: (i, j)),
        out_shape=jax.ShapeDtypeStruct((n, PROJ_W), F32),
        compiler_params=_params(("parallel", "arbitrary")),
        name="in_proj",
    )(h, w_packed)


def _pack_w_in(w):
    widths = (A_Q, A_KV, A_KV, IDX_HEADS * IDX_DIM, IDX_DIM, IDX_HEADS,
              B_CONV_CH, B_HEADS, B_HEADS, B_V, 2 * D_MODEL)
    offs = [0]
    for wd in widths:
        offs.append(offs[-1] + wd)
    qa, ka, va, qi, ki, wi, qkvb, ab, bb, zb, gates = [w[:, offs[i]:offs[i + 1]] for i in range(11)]
    misc_pad = jnp.zeros((D_MODEL, 128 - IDX_DIM - IDX_HEADS - 2 * B_HEADS), w.dtype)
    tail_pad = jnp.zeros((D_MODEL, PROJ_W - COL_MISC - 128), w.dtype)
    packed = jnp.concatenate([gates, qkvb, zb, qa, qi, ka, va, ki, wi, ab, bb, misc_pad, tail_pad], axis=1)
    return packed.astype(BF16)


def _attn_core(q_ref, qi_ref, wm_ref, kv_ref, ik_ref, o_ref, key_scr, bias_scr, *,
               qb, s_len, s_valid, t0):
    n_sel = float(TOPK_MAX)
    ik = ik_ref[0:s_len, MISC_KI:MISC_KI + IDX_DIM].astype(BF16)
    qi = qi_ref[...]
    wm = wm_ref[...]
    score = None
    for h in range(IDX_HEADS):
        s = _dot_nt(qi[:, h * IDX_DIM:(h + 1) * IDX_DIM].astype(BF16), ik)
        term = jnp.maximum(s, 0.0) * wm[:, MISC_WI + h:MISC_WI + h + 1]
        score = term if score is None else score + term
    score = score * (IDX_DIM ** -0.5)

    t_pos = t0 + lax.broadcasted_iota(I32, (qb, s_len), 0)
    k_pos = lax.broadcasted_iota(I32, (qb, s_len), 1)
    adm = (k_pos >> 6) <= (t_pos >> 6)
    if s_valid < s_len:
        adm = adm & (k_pos < s_valid)
    bits = pltpu.bitcast(score, I32)
    key = jnp.where(bits < 0, bits ^ jnp.int32(0x7FFFFFFF), bits)
    key = jnp.where(score == 0.0, jnp.int32(0), key)
    key_scr[:, 0:s_len] = jnp.where(adm, key, jnp.int32(INT_MIN))

    def bit_step(i, t):
        cand = t ^ lax.shift_left(jnp.int32(1), jnp.int32(31) - i)
        cnt = jnp.sum(jnp.where(key_scr[:, 0:s_len] >= cand, 1.0, 0.0), axis=1, keepdims=True)
        return jnp.where(cnt >= n_sel, cand, t)

    t = lax.fori_loop(0, 32, bit_step, jnp.full((qb, 1), INT_MIN, I32))
    keyed = key_scr[:, 0:s_len]
    n_ge = jnp.sum(jnp.where(keyed >= t, 1.0, 0.0), axis=1, keepdims=True)
    has_tie = jnp.where((n_ge > n_sel) & (t > jnp.int32(INT_MIN)), 1.0, 0.0)
    any_tie = jnp.max(has_tie) > 0.0
    neg_inf = jnp.float32(-jnp.inf)

    @pl.when(jnp.logical_not(any_tie))
    def _():
        t_eff = jnp.maximum(t, jnp.int32(INT_MIN + 1))
        bias_scr[:, 0:s_len] = jnp.where(key_scr[:, 0:s_len] >= t_eff, 0.0, neg_inf)

    @pl.when(any_tie)
    def _():
        kd = key_scr[:, 0:s_len]
        kp = lax.broadcasted_iota(I32, (qb, s_len), 1)
        n_gt = jnp.sum(jnp.where(kd > t, 1.0, 0.0), axis=1, keepdims=True)
        need = n_sel - n_gt
        eq = kd == t

        def pos_step(i, p):
            cand = p + lax.shift_left(jnp.int32(1), jnp.int32(12) - i)
            cnt = jnp.sum(jnp.where(eq, jnp.where(kp < cand, 1.0, 0.0), 0.0), axis=1, keepdims=True)
            return jnp.where(cnt <= need, cand, p)

        p = lax.fori_loop(0, 13, pos_step, jnp.zeros((qb, 1), I32))
        tie_bias = jnp.where(eq, jnp.where(kp < p, 0.0, neg_inf), neg_inf)
        sel_bias = jnp.where(kd > t, 0.0, tie_bias)
        bias_scr[:, 0:s_len] = jnp.where(kd == jnp.int32(INT_MIN), neg_inf, sel_bias)

    dist = jnp.abs(t_pos - k_pos).astype(F32)
    q = q_ref[...]
    for g in range(A_KV_HEADS):
        k_g = kv_ref[0:s_len, g * A_HEAD_DIM:(g + 1) * A_HEAD_DIM].astype(BF16)
        v_g = kv_ref[0:s_len, A_KV + g * A_HEAD_DIM:A_KV + (g + 1) * A_HEAD_DIM].astype(BF16)
        for r in range(A_HEADS // A_KV_HEADS):
            h = g * (A_HEADS // A_KV_HEADS) + r
            qh = (q[:, h * A_HEAD_DIM:(h + 1) * A_HEAD_DIM] * (A_HEAD_DIM ** -0.5)).astype(BF16)
            lg = _dot_nt(qh, k_g) - ALIBI_SLOPES[h] * dist + bias_scr[:, 0:s_len]
            m = jnp.max(lg, axis=1, keepdims=True)
            p = jnp.exp(lg - m)
            l = jnp.sum(p, axis=1, keepdims=True)
            o = _dot(p.astype(BF16), v_g) * (1.0 / l)
            o_ref[:, h * A_HEAD_DIM:(h + 1) * A_HEAD_DIM] = o.astype(o_ref.dtype)


def _attn_prompt_kernel(q_ref, qi_ref, wm_ref, kv_ref, ik_ref, o_ref, key_scr, bias_scr, *, qb, seq):
    j = pl.program_id(1)
    t0 = j * qb
    n_cls = 4 if seq % (4 * qb) == 0 and seq // qb >= 4 else 1
    step = seq // n_cls
    blocks_per_cls = step // qb
    for c in range(n_cls):
        @pl.when((j >= c * blocks_per_cls) & (j < (c + 1) * blocks_per_cls))
        def _(c=c):
            _attn_core(q_ref, qi_ref, wm_ref, kv_ref, ik_ref, o_ref, key_scr, bias_scr,
                       qb=qb, s_len=(c + 1) * step, s_valid=(c + 1) * step, t0=t0)


def _attn_sample_kernel(q_ref, qi_ref, wm_ref, kv_ref, ik_ref, o_ref, key_scr, bias_scr, *,
                        qb, s_pad, s_valid, q_start):
    _attn_core(q_ref, qi_ref, wm_ref, kv_ref.at[0], ik_ref.at[0], o_ref, key_scr, bias_scr,
               qb=qb, s_len=s_pad, s_valid=s_valid, t0=q_start)


def _attention_prompt(proj, bsz, seq):
    qb = Q_BLOCK if seq % Q_BLOCK == 0 else seq
    nb = seq // qb
    assert min(TOPK_MAX, seq // 4) == TOPK_MAX
    kern = functools.partial(_attn_prompt_kernel, qb=qb, seq=seq)
    return pl.pallas_call(
        kern,
        grid=(bsz, nb),
        in_specs=[pl.BlockSpec((qb, A_Q), lambda b, j: (b * nb + j, COL_QA // A_Q)),
                  pl.BlockSpec((qb, A_Q), lambda b, j: (b * nb + j, COL_QI // A_Q)),
                  pl.BlockSpec((qb, 128), lambda b, j: (b * nb + j, COL_MISC // 128)),
                  pl.BlockSpec((seq, 2 * A_KV), lambda b, j: (b, COL_KV // (2 * A_KV))),
                  pl.BlockSpec((seq, 128), lambda b, j: (b, COL_MISC // 128))],
        out_specs=pl.BlockSpec((qb, A_Q), lambda b, j: (b * nb + j, 0)),
        out_shape=jax.ShapeDtypeStruct((bsz * seq, A_Q), BF16),
        scratch_shapes=[pltpu.VMEM((qb, seq), I32), pltpu.VMEM((qb, seq), F32)],
        compiler_params=_params(("parallel", "arbitrary")),
        name="attn_prompt",
    )(proj, proj, proj, proj, proj)


def _attention_sample(proj, kv_all, ik_all, bsz, tlen, row0, s_valid):
    s_pad = kv_all.shape[1]
    assert tlen % Q_BLOCK != 0 and row0 % tlen == 0
    assert min(TOPK_MAX, s_valid // 4) == TOPK_MAX
    rb0 = row0 // tlen
    kern = functools.partial(_attn_sample_kernel, qb=tlen, s_pad=s_pad, s_valid=s_valid,
                             q_start=s_valid - tlen)
    return pl.pallas_call(
        kern,
        grid=(bsz,),
        in_specs=[pl.BlockSpec((tlen, A_Q), lambda b: (rb0 + b, COL_QA // A_Q)),
                  pl.BlockSpec((tlen, A_Q), lambda b: (rb0 + b, COL_QI // A_Q)),
                  pl.BlockSpec((tlen, 128), lambda b: (rb0 + b, COL_MISC // 128)),
                  pl.BlockSpec((1, s_pad, 2 * A_KV), lambda b: (b, 0, 0)),
                  pl.BlockSpec((1, s_pad, 128), lambda b: (b, 0, 0))],
        out_specs=pl.BlockSpec((tlen, A_Q), lambda b: (b, 0)),
        out_shape=jax.ShapeDtypeStruct((bsz * tlen, A_Q), BF16),
        scratch_shapes=[pltpu.VMEM((tlen, s_pad), I32), pltpu.VMEM((tlen, s_pad), F32)],
        compiler_params=_params(("parallel",)),
        name="attn_sample",
    )(proj, proj, proj, kv_all, ik_all)


def _unit_lower_inverse(m):
    c = m.shape[0]
    eye = (lax.broadcasted_iota(I32, (c, c), 0) == lax.broadcasted_iota(I32, (c, c), 1)).astype(F32)
    x = -m
    p = eye + x
    n = 2
    while n < c:
        x = _mm3(x, x)
        p = p + _mm3(p, x)
        n *= 2
    return p


def _gdn_kernel(qb_ref, kb_ref, vb_ref, z_ref, misc_ref, conv0_ref, s0_ref, convw_ref,
                alog_ref, dtb_ref, gn_ref, o_ref, s_out_ref, ext_scr, s_scr, *, c_len, n_chunks):
    c = pl.program_id(1)

    @pl.when(c == 0)
    def _():
        ext_scr[0:8, :] = conv0_ref[0]
        s_scr[...] = s0_ref[0]

    ext_scr[8:8 + c_len, 0:B_QK] = qb_ref[...]
    ext_scr[8:8 + c_len, B_QK:2 * B_QK] = kb_ref[...]
    ext_scr[8:8 + c_len, 2 * B_QK:B_CONV_CH] = vb_ref[...]
    w = convw_ref[...]
    y = None
    for j in range(CONV_WIDTH):
        off = 8 - (CONV_WIDTH - 1) + j
        term = ext_scr[off:off + c_len, :] * w[j:j + 1, :]
        y = term if y is None else y + term
    y = _silu(y)
    tail = ext_scr[c_len:c_len + 8, :]
    ext_scr[0:8, :] = tail

    misc = misc_ref[...]
    sp_in = misc + dtb_ref[...]
    softplus = jnp.maximum(sp_in, 0.0) + jnp.log1p(jnp.exp(-jnp.abs(sp_in)))
    g_tile = -jnp.exp(alog_ref[...]) * softplus
    beta_tile = _sigmoid(misc)

    row = lax.broadcasted_iota(I32, (c_len, c_len), 0)
    col = lax.broadcasted_iota(I32, (c_len, c_len), 1)
    incl = row >= col
    strict = row > col
    tril = incl.astype(BF16)
    g_hi = g_tile.astype(BF16)
    g_r1 = g_tile - g_hi.astype(F32)
    g_mid = g_r1.astype(BF16)
    g_lo = (g_r1 - g_mid.astype(F32)).astype(BF16)
    g_cum = _dot(tril, g_hi) + (_dot(tril, g_mid) + _dot(tril, g_lo))
    c_hi = g_cum.astype(BF16)
    c_r1 = g_cum - c_hi.astype(F32)
    c_mid = c_r1.astype(BF16)
    c_lo = (c_r1 - c_mid.astype(F32)).astype(BF16)
    lane_c = lax.broadcasted_iota(I32, (c_len, 128), 1)
    e_cum = jnp.exp(g_cum)

    for h in range(B_HEADS):
        qh = y[:, h * B_KEY_DIM:(h + 1) * B_KEY_DIM]
        kh = y[:, B_QK + h * B_KEY_DIM:B_QK + (h + 1) * B_KEY_DIM]
        vh = y[:, 2 * B_QK + h * B_VAL_DIM:2 * B_QK + (h + 1) * B_VAL_DIM]
        qh = qh * lax.rsqrt(jnp.sum(qh * qh, axis=-1, keepdims=True) + RMS_EPS)
        kh = kh * lax.rsqrt(jnp.sum(kh * kh, axis=-1, keepdims=True) + RMS_EPS)
        beta = beta_tile[:, MISC_B + h:MISC_B + h + 1]
        gc = g_cum[:, MISC_A + h:MISC_A + h + 1]
        pick = (lane_c == MISC_A + h).astype(BF16)
        gr = _dot_nt(pick, c_hi) + (_dot_nt(pick, c_mid) + _dot_nt(pick, c_lo))
        eg = e_cum[:, MISC_A + h:MISC_A + h + 1]
        g_last = g_cum[c_len - 1:c_len, MISC_A + h:MISC_A + h + 1]
        decay = jnp.where(incl, jnp.exp(jnp.where(incl, gc - gr, 0.0)), 0.0)
        kb = kh * beta
        m = jnp.where(strict, _mm1_nt(kb, kh) * decay, 0.0)
        t_inv = _unit_lower_inverse(m)
        rhs = jnp.concatenate([vh * beta, kb * eg], axis=1)
        sol = _mm3(t_inv, rhs)
        u = sol[:, 0:B_VAL_DIM]
        wk = sol[:, B_VAL_DIM:]
        qs = qh * (B_KEY_DIM ** -0.5)
        qk = jnp.where(incl, _mm1_nt(qs, kh) * decay, 0.0)
        qd = qs * eg
        kt = kh * jnp.exp(g_last - gc)
        s = s_scr[h]
        v_new = u - _mm1(wk, s)
        o = _mm1(qd, s) + _mm1(qk, v_new)
        s_scr[h] = s * jnp.exp(g_last) + _mm1_tn(kt, v_new)
        o = o * lax.rsqrt(jnp.mean(o * o, axis=-1, keepdims=True) + RMS_EPS) * gn_ref[...]
        o = o * _silu(z_ref[:, h * B_VAL_DIM:(h + 1) * B_VAL_DIM])
        o_ref[:, h * B_VAL_DIM:(h + 1) * B_VAL_DIM] = o.astype(o_ref.dtype)

    @pl.when(c == n_chunks - 1)
    def _():
        s_out_ref[0] = s_scr[...]


def _gated_deltanet(proj, conv0_pad, s0, conv_w, alog_row, dtb_row, gnorm_row, bsz, tlen, row0):
    c_len = CHUNK if tlen % CHUNK == 0 else tlen
    n_chunks = tlen // c_len
    assert row0 % c_len == 0 and c_len >= 8
    rb0 = row0 // c_len
    kern = functools.partial(_gdn_kernel, c_len=c_len, n_chunks=n_chunks)

    def rows(col_block):
        return pl.BlockSpec((c_len, B_QK), lambda b, c: (rb0 + b * n_chunks + c, col_block))

    const2 = lambda shape: pl.BlockSpec(shape, lambda b, c: (0, 0))
    return pl.pallas_call(
        kern,
        grid=(bsz, n_chunks),
        in_specs=[rows(COL_QB // B_QK), rows(COL_KB // B_QK), rows(COL_VB // B_QK), rows(COL_Z // B_QK),
                  pl.BlockSpec((c_len, 128), lambda b, c: (rb0 + b * n_chunks + c, COL_MISC // 128)),
                  pl.BlockSpec((1, 8, B_CONV_CH), lambda b, c: (b, 0, 0)),
                  pl.BlockSpec((1, B_HEADS, B_KEY_DIM, B_VAL_DIM), lambda b, c: (b, 0, 0, 0)),
                  const2((CONV_WIDTH, B_CONV_CH)), const2((1, 128)), const2((1, 128)),
                  const2((1, B_VAL_DIM))],
        out_specs=[pl.BlockSpec((c_len, B_V), lambda b, c: (b * n_chunks + c, 0)),
                   pl.BlockSpec((1, B_HEADS, B_KEY_DIM, B_VAL_DIM), lambda b, c: (b, 0, 0, 0))],
        out_shape=[jax.ShapeDtypeStruct((bsz * tlen, B_V), BF16),
                   jax.ShapeDtypeStruct((bsz, B_HEADS, B_KEY_DIM, B_VAL_DIM), F32)],
        scratch_shapes=[pltpu.VMEM((8 + c_len, B_CONV_CH), F32),
                        pltpu.VMEM((B_HEADS, B_KEY_DIM, B_VAL_DIM), F32)],
        compiler_params=_params(("parallel", "arbitrary")),
        name="gdn",
    )(proj, proj, proj, proj, proj, conv0_pad, s0, conv_w, alog_row, dtb_row, gnorm_row)


def _post_kernel(oa_ref, ob_ref, ga_ref, gb_ref, h_ref, woa_ref, wob_ref, wout_ref, g_ref, b_ref, o_ref):
    ya = _dot(oa_ref[...], woa_ref[...])
    yb = _dot(ob_ref[...], wob_ref[...])
    mix = _sigmoid(ga_ref[...]) * ya + _sigmoid(gb_ref[...]) * yb
    y = _dot(mix.astype(BF16), wout_ref[...])
    o_ref[...] = _ln(DEEPNORM_ALPHA * h_ref[...] + y, g_ref[...], b_ref[...])


def _post_mixer(o_a, o_b, proj, h, w_oa, w_ob, w_out, ln_g, ln_b):
    n = h.shape[0]
    tm = _pick_tile(n, (768, 512, 256, 128, 64, 32, 16))
    row = lambda w, cb: pl.BlockSpec((tm, w), lambda i: (i, cb))
    full = lambda a, b: pl.BlockSpec((a, b), lambda i: (0, 0))
    return pl.pallas_call(
        _post_kernel,
        grid=(n // tm,),
        in_specs=[row(A_Q, 0), row(B_V, 0), row(D_MODEL, COL_GA // D_MODEL), row(D_MODEL, COL_GB // D_MODEL),
                  row(D_MODEL, 0), full(A_Q, D_MODEL), full(B_V, D_MODEL), full(D_MODEL, D_MODEL),
                  full(1, D_MODEL), full(1, D_MODEL)],
        out_specs=row(D_MODEL, 0),
        out_shape=jax.ShapeDtypeStruct((n, D_MODEL), F32),
        compiler_params=_params(("parallel",)),
        name="post_mixer",
    )(o_a, o_b, proj, proj, h, w_oa, w_ob, w_out, ln_g.reshape(1, D_MODEL), ln_b.reshape(1, D_MODEL))


def _ffn_kernel(h_ref, w1_ref, w3_ref, w2_ref, g_ref, b_ref, o_ref, acc_ref, *, n_ff):
    j = pl.program_id(1)
    hb = h_ref[...].astype(BF16)
    a = _dot(hb, w1_ref[...])
    b = _dot(hb, w3_ref[...])
    part = _dot((_silu(a) * b).astype(BF16), w2_ref[...])

    @pl.when(j == 0)
    def _():
        acc_ref[...] = part

    @pl.when(j > 0)
    def _():
        acc_ref[...] += part

    @pl.when(j == n_ff - 1)
    def _():
        o_ref[...] = _ln(DEEPNORM_ALPHA * h_ref[...] + acc_ref[...], g_ref[...], b_ref[...])


def _dense_ffn(h, w1, w3, w2, ln_g, ln_b):
    n = h.shape[0]
    tm = _pick_tile(n, (768, 512, 256, 128, 64, 32, 16))
    tf = D_FF // 2
    n_ff = D_FF // tf
    return pl.pallas_call(
        functools.partial(_ffn_kernel, n_ff=n_ff),
        grid=(n // tm, n_ff),
        in_specs=[pl.BlockSpec((tm, D_MODEL), lambda i, j: (i, 0)),
                  pl.BlockSpec((D_MODEL, tf), lambda i, j: (0, j)),
                  pl.BlockSpec((D_MODEL, tf), lambda i, j: (0, j)),
                  pl.BlockSpec((tf, D_MODEL), lambda i, j: (j, 0)),
                  pl.BlockSpec((1, D_MODEL), lambda i, j: (0, 0)),
                  pl.BlockSpec((1, D_MODEL), lambda i, j: (0, 0))],
        out_specs=pl.BlockSpec((tm, D_MODEL), lambda i, j: (i, 0)),
        out_shape=jax.ShapeDtypeStruct((n, D_MODEL), F32),
        scratch_shapes=[pltpu.VMEM((tm, D_MODEL), F32)],
        compiler_params=_params(("parallel", "arbitrary")),
        name="dense_ffn",
    )(h, w1, w3, w2, ln_g.reshape(1, D_MODEL), ln_b.reshape(1, D_MODEL))


def _router_kernel(h_ref, r_ref, gate_ref):
    logits = _mm3(h_ref[...], r_ref[...])
    lane = lax.broadcasted_iota(I32, logits.shape, 1)
    neg_inf = jnp.float32(-jnp.inf)
    lg = jnp.where(lane < N_EXPERTS, logits, neg_inf)
    v1 = jnp.max(lg, axis=-1, keepdims=True)
    i1 = jnp.min(jnp.where(lg == v1, lane, 128), axis=-1, keepdims=True)
    lg2 = jnp.where(lane == i1, neg_inf, lg)
    v2 = jnp.max(lg2, axis=-1, keepdims=True)
    i2 = jnp.min(jnp.where(lg2 == v2, lane, 128), axis=-1, keepdims=True)
    e2 = jnp.exp(v2 - v1)
    w1 = 1.0 / (1.0 + e2)
    w2 = e2 / (1.0 + e2)
    gate_ref[...] = jnp.where(lane == i1, w1, jnp.where(lane == i2, w2, 0.0))


def _router(h, router_pad):
    n = h.shape[0]
    tm = _pick_tile(n, (768, 512, 256, 128, 64, 32, 16, 8))
    return pl.pallas_call(
        _router_kernel,
        grid=(n // tm,),
        in_specs=[pl.BlockSpec((tm, D_MODEL), lambda i: (i, 0)),
                  pl.BlockSpec((D_MODEL, 128), lambda i: (0, 0))],
        out_specs=pl.BlockSpec((tm, 128), lambda i: (i, 0)),
        out_shape=jax.ShapeDtypeStruct((n, 128), F32),
        compiler_params=_params(("parallel",)),
        name="moe_router",
    )(h, router_pad)


def _expert_kernel(te_ref, nt_ref, x_ref, w1_ref, w3_ref, w2_ref, o_ref, acc_ref, *, n_ff):
    i = pl.program_id(0)
    j = pl.program_id(1)

    @pl.when(i < nt_ref[0])
    def _():
        x = x_ref[...]
        a = _dot(x, w1_ref[0])
        b = _dot(x, w3_ref[0])
        part = _dot((_silu(a) * b).astype(BF16), w2_ref[0])

        @pl.when(j == 0)
        def _():
            acc_ref[...] = part

        @pl.when(j > 0)
        def _():
            acc_ref[...] += part

        @pl.when(j == n_ff - 1)
        def _():
            o_ref[...] = acc_ref[...]


def _grouped_experts(x_sorted, tile_expert, n_used, w1, w3, w2, tm):
    n_tiles = x_sorted.shape[0] // tm
    tf = D_FF_EXPERT // 2
    n_ff = D_FF_EXPERT // tf

    def row_map(i, j, te, nt):
        return (jnp.minimum(i, nt[0] - 1), 0)

    def ff_of(i, j, nt):
        return jnp.where(i < nt[0], j, n_ff - 1)

    grid_spec = pltpu.PrefetchScalarGridSpec(
        num_scalar_prefetch=2,
        grid=(n_tiles, n_ff),
        in_specs=[pl.BlockSpec((tm, D_MODEL), row_map),
                  pl.BlockSpec((1, D_MODEL, tf), lambda i, j, te, nt: (te[i], 0, ff_of(i, j, nt))),
                  pl.BlockSpec((1, D_MODEL, tf), lambda i, j, te, nt: (te[i], 0, ff_of(i, j, nt))),
                  pl.BlockSpec((1, tf, D_MODEL), lambda i, j, te, nt: (te[i], ff_of(i, j, nt), 0))],
        out_specs=pl.BlockSpec((tm, D_MODEL), row_map),
        scratch_shapes=[pltpu.VMEM((tm, D_MODEL), F32)],
    )
    return pl.pallas_call(
        functools.partial(_expert_kernel, n_ff=n_ff),
        grid_spec=grid_spec,
        out_shape=jax.ShapeDtypeStruct((n_tiles * tm, D_MODEL), F32),
        compiler_params=_params(("arbitrary", "arbitrary")),
        name="moe_experts",
    )(tile_expert, n_used, x_sorted, w1, w3, w2)


def _combine_kernel(h_ref, y1_ref, y2_ref, g_ref, b_ref, o_ref):
    f = y1_ref[...] + y2_ref[...]
    o_ref[...] = _ln(DEEPNORM_ALPHA * h_ref[...] + f, g_ref[...], b_ref[...])


def _combine_ln(h, y1, y2, ln_g, ln_b):
    n = h.shape[0]
    tm = _pick_tile(n, (768, 512, 256, 128, 64, 32, 16, 8))
    row = pl.BlockSpec((tm, D_MODEL), lambda i: (i, 0))
    vec = pl.BlockSpec((1, D_MODEL), lambda i: (0, 0))
    return pl.pallas_call(
        _combine_kernel,
        grid=(n // tm,),
        in_specs=[row, row, row, vec, vec],
        out_specs=row,
        out_shape=jax.ShapeDtypeStruct((n, D_MODEL), F32),
        compiler_params=_params(("parallel",)),
        name="moe_combine",
    )(h, y1, y2, ln_g.reshape(1, D_MODEL), ln_b.reshape(1, D_MODEL))


def _moe_ffn(h, router, w1, w3, w2, ln_g, ln_b):
    n = h.shape[0]
    router_pad = jnp.zeros((D_MODEL, 128), F32).at[:, :N_EXPERTS].set(router)
    gate = _router(h, router_pad)[:, :N_EXPERTS]
    tm = 512 if n >= 4096 else 64
    top_w, top_e = lax.top_k(gate, 2)
    flat_e = top_e.reshape(-1).astype(I32)
    flat_w = top_w.reshape(-1)
    order = jnp.argsort(flat_e, stable=True).astype(I32)
    counts = jnp.sum(jax.nn.one_hot(flat_e, N_EXPERTS, dtype=I32), axis=0)
    padded = ((counts + tm - 1) // tm) * tm
    starts = jnp.cumsum(counts) - counts
    pstarts = jnp.cumsum(padded) - padded
    n_tiles = (2 * n) // tm + N_EXPERTS
    sorted_e = flat_e[order]
    rank = jnp.arange(2 * n, dtype=I32) - starts[sorted_e]
    dest = pstarts[sorted_e] + rank
    src_tok = jnp.zeros((n_tiles * tm,), I32).at[dest].set(order // 2)
    row_of = jnp.zeros((2 * n,), I32).at[order].set(dest)
    tile_start = jnp.arange(n_tiles, dtype=I32) * tm
    pend = jnp.cumsum(padded)
    n_used = (pend[-1] // tm).astype(I32).reshape(1)
    last_start = jnp.minimum(tile_start, (n_used[0] - 1) * tm)
    tile_expert = jnp.sum((last_start[:, None] >= pend[None, :]).astype(I32), axis=1).astype(I32)
    x_sorted = jnp.take(h.astype(BF16), src_tok, axis=0)
    y_sorted = _grouped_experts(x_sorted, tile_expert, n_used, w1, w3, w2, tm)
    rows = row_of.reshape(n, 2)
    y1 = jnp.take(y_sorted, rows[:, 0], axis=0) * flat_w.reshape(n, 2)[:, 0:1]
    y2 = jnp.take(y_sorted, rows[:, 1], axis=0) * flat_w.reshape(n, 2)[:, 1:2]
    return _combine_ln(h, y1, y2, ln_g, ln_b)


def kernel(x_prompt, x_sample, cache_k, cache_v, cache_idx_k, state_conv, state_gdn,
           ln_in_g, ln_in_b, w_in, conv_w, a_log, dt_bias, gnorm_g, w_oa, w_ob, w_out,
           ln1_g, ln1_b, ln2_g, ln2_b, ffn_w1, ffn_w3, ffn_w2,
           moe_router, moe_w1, moe_w3, moe_w2):
    bp, tp, _ = x_prompt.shape
    bs, ts, _ = x_sample.shape
    past = cache_k.shape[2]
    n_p = bp * tp
    n_s = bs * ts
    x = jnp.concatenate([x_prompt.reshape(n_p, D_MODEL), x_sample.reshape(n_s, D_MODEL)], axis=0)
    h = _layer_norm_rows(x, ln_in_g, ln_in_b)

    s_valid = past + ts
    s_pad = ((s_valid + 127) // 128) * 128
    lane_pad = jnp.zeros((1, 128), F32)
    conv0_p = jnp.zeros((bp, 8, B_CONV_CH), F32)
    s0_p = jnp.zeros((bp, B_HEADS, B_KEY_DIM, B_VAL_DIM), F32)

    outs = {k: [] for k in ("kp", "vp", "ikp", "convp", "sp", "ks", "vs", "iks", "convs", "ss")}
    for l in range(DEPTH):
        proj = _project(h, _pack_w_in(w_in[l]))
        k_new = proj[:, COL_KV:COL_KV + A_KV]
        v_new = proj[:, COL_KV + A_KV:COL_KV + 2 * A_KV]
        ik_new = proj[:, COL_MISC + MISC_KI:COL_MISC + MISC_KI + IDX_DIM]
        qkv_pre = proj[:, COL_QB:COL_QB + B_CONV_CH]

        oa_p = _attention_prompt(proj, bp, tp)
        kv_s = proj[n_p:, COL_KV:COL_KV + 2 * A_KV].reshape(bs, ts, 2 * A_KV)
        kv_cache = jnp.concatenate([cache_k[l].reshape(bs, past, A_KV), cache_v[l].reshape(bs, past, A_KV)], axis=-1)
        kv_all = jnp.concatenate([kv_cache, kv_s, jnp.zeros((bs, s_pad - s_valid, 2 * A_KV), F32)], axis=1)
        ik_s = proj[n_p:, COL_MISC:COL_MISC + 128].reshape(bs, ts, 128)
        ik_cache = jnp.pad(cache_idx_k[l], ((0, 0), (0, 0), (0, 128 - IDX_DIM)))
        ik_all = jnp.concatenate([ik_cache, ik_s, jnp.zeros((bs, s_pad - s_valid, 128), F32)], axis=1)
        oa_s = _attention_sample(proj, kv_all, ik_all, bs, ts, n_p, s_valid)
        o_a = jnp.concatenate([oa_p, oa_s], axis=0)

        alog_row = lax.dynamic_update_slice(lane_pad, a_log[l].reshape(1, B_HEADS), (0, MISC_A))
        dtb_row = lax.dynamic_update_slice(lane_pad, dt_bias[l].reshape(1, B_HEADS), (0, MISC_A))
        gn_row = gnorm_g[l].reshape(1, B_VAL_DIM)
        conv0_s = jnp.pad(state_conv[l], ((0, 0), (8 - (CONV_WIDTH - 1), 0), (0, 0)))
        ob_p, s_p = _gated_deltanet(proj, conv0_p, s0_p, conv_w[l], alog_row, dtb_row, gn_row, bp, tp, 0)
        ob_s, s_s = _gated_deltanet(proj, conv0_s, state_gdn[l], conv_w[l], alog_row, dtb_row, gn_row,
                                    bs, ts, n_p)
        o_b = jnp.concatenate([ob_p, ob_s], axis=0)

        h = _post_mixer(o_a, o_b, proj, h, w_oa[l].astype(BF16), w_ob[l].astype(BF16),
                        w_out[l].astype(BF16), ln1_g[l], ln1_b[l])
        if l % 2 == 0:
            i = l // 2
            h = _dense_ffn(h, ffn_w1[i].astype(BF16), ffn_w3[i].astype(BF16), ffn_w2[i].astype(BF16),
                           ln2_g[l], ln2_b[l])
        else:
            i = l // 2
            h = _moe_ffn(h, moe_router[i], moe_w1[i].astype(BF16), moe_w3[i].astype(BF16),
                         moe_w2[i].astype(BF16), ln2_g[l], ln2_b[l])

        outs["kp"].append(k_new[:n_p].reshape(bp, tp, A_KV_HEADS, A_HEAD_DIM))
        outs["vp"].append(v_new[:n_p].reshape(bp, tp, A_KV_HEADS, A_HEAD_DIM))
        outs["ikp"].append(ik_new[:n_p].reshape(bp, tp, IDX_DIM))
        outs["convp"].append(qkv_pre[:n_p].reshape(bp, tp, B_CONV_CH)[:, tp - (CONV_WIDTH - 1):])
        outs["sp"].append(s_p)
        outs["ks"].append(k_new[n_p:].reshape(bs, ts, A_KV_HEADS, A_HEAD_DIM))
        outs["vs"].append(v_new[n_p:].reshape(bs, ts, A_KV_HEADS, A_HEAD_DIM))
        outs["iks"].append(ik_new[n_p:].reshape(bs, ts, IDX_DIM))
        outs["convs"].append(qkv_pre[n_p:].reshape(bs, ts, B_CONV_CH)[:, ts - (CONV_WIDTH - 1):])
        outs["ss"].append(s_s)

    st = lambda name: jnp.stack(outs[name])
    return (h[:n_p].reshape(bp, tp, D_MODEL), h[n_p:].reshape(bs, ts, D_MODEL),
            st("kp"), st("vp"), st("ikp"), st("convp"), st("sp"),
            st("ks"), st("vs"), st("iks"), st("convs"), st("ss"))
```

```python
import functools

import jax
import jax.numpy as jnp
from jax import lax
from jax.experimental import pallas as pl
from jax.experimental.pallas import tpu as pltpu

F32 = jnp.float32
BF16 = jnp.bfloat16
I32 = jnp.int32

D_MODEL = 1024
DEPTH = 2
CHUNK = 64
Q_BLOCK = 128
A_HEADS = 8
A_KV_HEADS = 2
A_HEAD_DIM = 64
IDX_HEADS = 8
IDX_DIM = 64
TOPK_MAX = 256
B_HEADS = 4
B_KEY_DIM = 128
B_VAL_DIM = 128
CONV_WIDTH = 4
D_FF = 2816
N_EXPERTS = 8
D_FF_EXPERT = 3584
LN_EPS = 1e-5
RMS_EPS = 1e-6
DEEPNORM_ALPHA = (2 * DEPTH) ** 0.25
ALIBI_SLOPES = tuple(2.0 ** (-8.0 * (h + 1) / A_HEADS) for h in range(A_HEADS))

A_Q = A_HEADS * A_HEAD_DIM
A_KV = A_KV_HEADS * A_HEAD_DIM
A_GROUP = A_HEADS // A_KV_HEADS
B_QK = B_HEADS * B_KEY_DIM
B_V = B_HEADS * B_VAL_DIM
B_CONV_CH = 2 * B_QK + B_V

COL_GA = 0
COL_GB = 1024
COL_QB = 2048
COL_KB = 2560
COL_VB = 3072
COL_Z = 3584
COL_QA = 4096
COL_KV = 4608
COL_MISC = 4864
PROJ_W = 5120
PROJ_TN = 1280
IDX_COL_QI = 0
IDX_COL_MISC = IDX_HEADS * IDX_DIM
IDX_W = IDX_COL_MISC + 128
MISC_KI = 0
MISC_WI = 64
MISC_A = 72
MISC_B = 76
IDX_K_AUG = 4 * IDX_DIM

INT_MIN = -(2 ** 31)
VMEM_LIMIT = 56 * 1024 * 1024
VT_ROWS = 80


def _pick_tile(n, candidates):
    for c in candidates:
        if n % c == 0:
            return c
    raise ValueError(f"no tile for {n}")


def _params(sem):
    return pltpu.CompilerParams(dimension_semantics=sem, vmem_limit_bytes=VMEM_LIMIT)


def _ln(x, g, b):
    mu = jnp.mean(x, axis=-1, keepdims=True)
    xc = x - mu
    var = jnp.mean(xc * xc, axis=-1, keepdims=True)
    return xc * lax.rsqrt(var + LN_EPS) * g + b


def _dot(a, b):
    return jnp.dot(a, b, preferred_element_type=F32)


def _dot_nt(a, b):
    return lax.dot_general(a, b, (((1,), (1,)), ((), ())), preferred_element_type=F32)


def _mm1(a, b):
    return _dot(a.astype(BF16), b.astype(BF16))


def _mm1_nt(a, b):
    return _dot_nt(a.astype(BF16), b.astype(BF16))


def _split2(x):
    hi = x.astype(BF16)
    lo = (x - hi.astype(F32)).astype(BF16)
    return hi, lo


def _split3(x):
    hi = x.astype(BF16)
    r1 = x - hi.astype(F32)
    mid = r1.astype(BF16)
    lo = (r1 - mid.astype(F32)).astype(BF16)
    return hi, mid, lo


def _mm3(a, b):
    ah, al = _split2(a)
    bh, bl = _split2(b)
    return _dot(ah, bh) + (_dot(ah, bl) + _dot(al, bh))


def _eye(n, m, dtype):
    return (lax.broadcasted_iota(I32, (n, m), 0) == lax.broadcasted_iota(I32, (n, m), 1)).astype(dtype)


def _silu(x):
    return x * (1.0 / (1.0 + jnp.exp(-x)))


def _sigmoid(x):
    return 1.0 / (1.0 + jnp.exp(-x))


def _any_spec():
    return pl.BlockSpec(memory_space=pl.ANY)


def _ln_kernel(x_ref, g_ref, b_ref, o_ref):
    o_ref[...] = _ln(x_ref[...], g_ref[...], b_ref[...])


def _ln_alias_kernel(x_ref, g_ref, b_ref, buf_ref, o_ref):
    del buf_ref
    o_ref[...] = _ln(x_ref[...], g_ref[...], b_ref[...])


def _layer_norm_streams(x_p, x_s, g, b):
    n_p, n_s = x_p.shape[0], x_s.shape[0]
    n = n_p + n_s
    tm = _pick_tile(n_p, (1024, 512, 256, 128, 64, 32, 16, 8))
    assert n_p % n_s == 0
    vec = lambda: pl.BlockSpec((1, D_MODEL), lambda i: (0, 0))
    g2, b2 = g.reshape(1, D_MODEL), b.reshape(1, D_MODEL)
    h = pl.pallas_call(
        _ln_kernel,
        grid=(n_p // tm,),
        in_specs=[pl.BlockSpec((tm, D_MODEL), lambda i: (i, 0)), vec(), vec()],
        out_specs=pl.BlockSpec((tm, D_MODEL), lambda i: (i, 0)),
        out_shape=jax.ShapeDtypeStruct((n, D_MODEL), F32),
        compiler_params=_params(("parallel",)),
        name="ln_in_prompt",
    )(x_p, g2, b2)
    return pl.pallas_call(
        _ln_alias_kernel,
        grid=(1,),
        in_specs=[pl.BlockSpec((n_s, D_MODEL), lambda i: (0, 0)), vec(), vec(), _any_spec()],
        out_specs=pl.BlockSpec((n_s, D_MODEL), lambda i: (n_p // n_s, 0)),
        out_shape=jax.ShapeDtypeStruct((n, D_MODEL), F32),
        input_output_aliases={3: 0},
        compiler_params=_params(("arbitrary",)),
        name="ln_in_sample",
    )(x_s, g2, b2, h)


def _proj_kernel(h_ref, w_ref, o_ref, k_ref, v_ref, *, kv_tile):
    res = _dot(h_ref[...].astype(BF16), w_ref[...])
    o_ref[...] = res

    @pl.when(pl.program_id(1) == kv_tile)
    def _():
        c0 = COL_KV - kv_tile * PROJ_TN
        k_ref[...] = res[:, c0:c0 + A_KV]
        v_ref[...] = res[:, c0 + A_KV:c0 + 2 * A_KV]


def _proj_idx_kernel(h_ref, wh_ref, wl_ref, o_ref, ik_ref):
    hh, hl = _split2(h_ref[...])
    res = _dot(hh, wh_ref[...]) + (_dot(hh, wl_ref[...]) + _dot(hl, wh_ref[...]))
    o_ref[...] = res
    ik_ref[...] = res[:, IDX_COL_MISC + MISC_KI:IDX_COL_MISC + MISC_KI + IDX_DIM]


def _project_idx(h, w_hi, w_lo):
    n = h.shape[0]
    tm = _pick_tile(n, (768, 512, 256, 128, 64, 32, 16, 8))
    return pl.pallas_call(
        _proj_idx_kernel,
        grid=(n // tm,),
        in_specs=[pl.BlockSpec((tm, D_MODEL), lambda i: (i, 0)),
                  pl.BlockSpec((D_MODEL, IDX_W), lambda i: (0, 0)),
                  pl.BlockSpec((D_MODEL, IDX_W), lambda i: (0, 0))],
        out_specs=[pl.BlockSpec((tm, IDX_W), lambda i: (i, 0)),
                   pl.BlockSpec((tm, IDX_DIM), lambda i: (i, 0))],
        out_shape=[jax.ShapeDtypeStruct((n, IDX_W), F32),
                   jax.ShapeDtypeStruct((n, IDX_DIM), F32)],
        compiler_params=_params(("parallel",)),
        name="idx_proj",
    )(h, w_hi, w_lo)


def _project(h, w_packed):
    n = h.shape[0]
    tm = _pick_tile(n, (768, 512, 256, 128, 64, 32, 16, 8))
    kv_tile = COL_KV // PROJ_TN
    assert (COL_KV + 2 * A_KV - 1) // PROJ_TN == kv_tile
    return pl.pallas_call(
        functools.partial(_proj_kernel, kv_tile=kv_tile),
        grid=(n // tm, PROJ_W // PROJ_TN),
        in_specs=[pl.BlockSpec((tm, D_MODEL), lambda i, j: (i, 0)),
                  pl.BlockSpec((D_MODEL, PROJ_TN), lambda i, j: (0, j))],
        out_specs=[pl.BlockSpec((tm, PROJ_TN), lambda i, j: (i, j)),
                   pl.BlockSpec((tm, A_KV), lambda i, j: (i, 0)),
                   pl.BlockSpec((tm, A_KV), lambda i, j: (i, 0))],
        out_shape=[jax.ShapeDtypeStruct((n, PROJ_W), F32),
                   jax.ShapeDtypeStruct((n, A_KV), F32),
                   jax.ShapeDtypeStruct((n, A_KV), F32)],
        compiler_params=_params(("parallel", "arbitrary")),
        name="in_proj",
    )(h, w_packed)


def _pack_w_in(w):
    widths = (A_Q, A_KV, A_KV, IDX_HEADS * IDX_DIM, IDX_DIM, IDX_HEADS,
              B_CONV_CH, B_HEADS, B_HEADS, B_V, 2 * D_MODEL)
    offs = [0]
    for wd in widths:
        offs.append(offs[-1] + wd)
    qa, ka, va, qi, ki, wi, qkvb, ab, bb, zb, gates = [w[:, offs[i]:offs[i + 1]] for i in range(11)]
    zeros = lambda k: jnp.zeros((D_MODEL, k), w.dtype)
    packed = jnp.concatenate([gates, qkvb, zb, qa, ka, va, zeros(MISC_A), ab, bb,
                              zeros(128 - MISC_B - B_HEADS), zeros(PROJ_W - COL_MISC - 128)], axis=1)
    idx = jnp.concatenate([qi, ki, wi, zeros(128 - MISC_WI - IDX_HEADS)], axis=1)
    idx_hi = idx.astype(BF16)
    idx_lo = (idx - idx_hi.astype(F32)).astype(BF16)
    return packed.astype(BF16), idx_hi, idx_lo


def _col_reduce(get, s_len, qb, reduce_fn, pair_fn, n_acc=8):
    while s_len % (8 * n_acc) != 0:
        n_acc //= 2
    blk = s_len // n_acc
    parts = [reduce_fn(get(a * blk, (a + 1) * blk).reshape(blk // 8, 8, qb), axis=0) for a in range(n_acc)]
    while len(parts) > 1:
        parts = [pair_fn(parts[k], parts[k + 1]) for k in range(0, len(parts), 2)]
    return reduce_fn(parts[0], axis=0, keepdims=True)


def _attn_prep(kv_ref, ik_ref, ikb_scr, kg_scr, vt_scr, s_len):
    ik_hi, ik_lo = _split2(ik_ref[0:s_len, MISC_KI:MISC_KI + IDX_DIM])
    ikb_scr[0:s_len, :] = jnp.concatenate([ik_hi, ik_lo, ik_hi, jnp.zeros_like(ik_hi)], axis=1)
    pick = _eye(VT_ROWS, A_HEAD_DIM, BF16)
    ones_row = (lax.broadcasted_iota(I32, (VT_ROWS, s_len), 0) == A_HEAD_DIM).astype(F32)
    for g in range(A_KV_HEADS):
        kg_scr[g, 0:s_len, :] = kv_ref[0:s_len, g * A_HEAD_DIM:(g + 1) * A_HEAD_DIM].astype(BF16)
        v_g = kv_ref[0:s_len, A_KV + g * A_HEAD_DIM:A_KV + (g + 1) * A_HEAD_DIM].astype(BF16)
        vt_scr[g, :, 0:s_len] = (_dot_nt(pick, v_g) + ones_row).astype(BF16)


def _attn_core(q_ref, qi_ref, wm_ref, ikb_scr, kg_scr, vt_scr, o_ref, key_scr, bias_scr, *,
               qb, s_len, s_valid, t0):
    n_sel = float(TOPK_MAX)
    w_t = None
    for piece in _split3(wm_ref[...]):
        part = _dot_nt(_eye(128, 128, BF16), piece)
        w_t = part if w_t is None else w_t + part
    qi = qi_ref[...]
    ik = ikb_scr[0:s_len, :]
    score = None
    for h in range(IDX_HEADS):
        q_hi, q_lo = _split2(qi[:, h * IDX_DIM:(h + 1) * IDX_DIM])
        s = _dot_nt(ik, jnp.concatenate([q_hi, q_hi, q_lo, jnp.zeros_like(q_hi)], axis=1))
        term = jnp.maximum(s, 0.0) * w_t[MISC_WI + h:MISC_WI + h + 1, :]
        score = term if score is None else score + term
    score = score * (IDX_DIM ** -0.5)

    k_pos = lax.broadcasted_iota(I32, (s_len, qb), 0)
    t_pos = t0 + lax.broadcasted_iota(I32, (s_len, qb), 1)
    adm = (k_pos >> 6) <= (t_pos >> 6)
    if s_valid < s_len:
        adm = adm & (k_pos < s_valid)
    bits = pltpu.bitcast(score, I32)
    key = jnp.where(bits < 0, bits ^ jnp.int32(0x7FFFFFFF), bits)
    key = jnp.where(score == 0.0, jnp.int32(0), key)
    key_scr[0:s_len, :] = jnp.where(adm, key, jnp.int32(INT_MIN))

    def count_ge(thr):
        return _col_reduce(lambda lo, hi: jnp.where(key_scr[lo:hi, :] >= thr, 1.0, 0.0), s_len, qb, jnp.sum, jnp.add)

    def bit_step(i, t):
        cand = t ^ lax.shift_left(jnp.int32(1), jnp.int32(31) - i)
        return jnp.where(count_ge(cand) >= n_sel, cand, t)

    t = lax.fori_loop(0, 32, bit_step, jnp.full((1, qb), INT_MIN, I32))
    n_ge = count_ge(t)
    has_tie = jnp.where((n_ge > n_sel) & (t > jnp.int32(INT_MIN)), 1.0, 0.0)
    any_tie = jnp.max(has_tie) > 0.0
    neg_inf = jnp.float32(-jnp.inf)

    @pl.when(jnp.logical_not(any_tie))
    def _():
        t_eff = jnp.maximum(t, jnp.int32(INT_MIN + 1))
        bias_scr[0:s_len, :] = jnp.where(key_scr[0:s_len, :] >= t_eff, 0.0, neg_inf)

    @pl.when(any_tie)
    def _():
        kd = key_scr[0:s_len, :]
        kp = lax.broadcasted_iota(I32, (s_len, qb), 0)
        n_gt = jnp.sum(jnp.where(kd > t, 1.0, 0.0), axis=0, keepdims=True)
        need = n_sel - n_gt
        eq = kd == t

        def pos_step(i, p):
            cand = p + lax.shift_left(jnp.int32(1), jnp.int32(12) - i)
            cnt = jnp.sum(jnp.where(eq, jnp.where(kp < cand, 1.0, 0.0), 0.0), axis=0, keepdims=True)
            return jnp.where(cnt <= need, cand, p)

        p = lax.fori_loop(0, 13, pos_step, jnp.zeros((1, qb), I32))
        tie_bias = jnp.where(eq, jnp.where(kp < p, 0.0, neg_inf), neg_inf)
        sel_bias = jnp.where(kd > t, 0.0, tie_bias)
        bias_scr[0:s_len, :] = jnp.where(kd == jnp.int32(INT_MIN), neg_inf, sel_bias)

    dist = jnp.abs(t_pos - k_pos).astype(F32)
    q = q_ref[...]
    o_parts = []
    for g in range(A_KV_HEADS):
        for r in range(A_GROUP):
            h = g * A_GROUP + r
            qh = (q[:, h * A_HEAD_DIM:(h + 1) * A_HEAD_DIM] * (A_HEAD_DIM ** -0.5)).astype(BF16)
            lg = _dot_nt(kg_scr[g, 0:s_len, :], qh) - ALIBI_SLOPES[h] * dist + bias_scr[0:s_len, :]
            m = _col_reduce(lambda lo, hi: lg[lo:hi, :], s_len, qb, jnp.max, jnp.maximum)
            p = jnp.exp(lg - m).astype(BF16)
            ov = _dot(vt_scr[g, :, 0:s_len], p)
            o_parts.append(ov[0:A_HEAD_DIM] * (1.0 / ov[A_HEAD_DIM:A_HEAD_DIM + 1]))
    o_t = jnp.concatenate(o_parts, axis=0).astype(BF16)
    o_ref[...] = _dot_nt(_eye(qb, qb, BF16), o_t).astype(o_ref.dtype)


def _attn_prompt_kernel(q_ref, qi_ref, wm_ref, kv_ref, ik_ref, o_ref,
                        ikb_scr, kg_scr, vt_scr, key_scr, bias_scr, *, qb, seq):
    j = pl.program_id(1)

    @pl.when(j == 0)
    def _():
        _attn_prep(kv_ref, ik_ref, ikb_scr, kg_scr, vt_scr, seq)

    n_cls = 4 if seq % (4 * qb) == 0 and seq // qb >= 4 else 1
    step = seq // n_cls
    blocks_per_cls = step // qb
    for c in range(n_cls):
        @pl.when((j >= c * blocks_per_cls) & (j < (c + 1) * blocks_per_cls))
        def _(c=c):
            _attn_core(q_ref, qi_ref, wm_ref, ikb_scr, kg_scr, vt_scr, o_ref, key_scr, bias_scr,
                       qb=qb, s_len=(c + 1) * step, s_valid=(c + 1) * step, t0=j * qb)


def _attn_sample_kernel(q_ref, qi_ref, wm_ref, kv_ref, ik_ref, o_ref,
                        ikb_scr, kg_scr, vt_scr, key_scr, bias_scr, *, qb, s_pad, s_valid, q_start):
    _attn_prep(kv_ref.at[0], ik_ref.at[0], ikb_scr, kg_scr, vt_scr, s_pad)
    _attn_core(q_ref, qi_ref, wm_ref, ikb_scr, kg_scr, vt_scr, o_ref, key_scr, bias_scr,
               qb=qb, s_len=s_pad, s_valid=s_valid, t0=q_start)


def _attn_scratch(qb, s):
    return [pltpu.VMEM((s, IDX_K_AUG), BF16), pltpu.VMEM((A_KV_HEADS, s, A_HEAD_DIM), BF16),
            pltpu.VMEM((A_KV_HEADS, VT_ROWS, s), BF16), pltpu.VMEM((s, qb), I32), pltpu.VMEM((s, qb), F32)]


def _attention_prompt(proj, idxp, bsz, seq):
    qb = Q_BLOCK if seq % Q_BLOCK == 0 else seq
    nb = seq // qb
    assert min(TOPK_MAX, seq // 4) == TOPK_MAX
    kern = functools.partial(_attn_prompt_kernel, qb=qb, seq=seq)
    return pl.pallas_call(
        kern,
        grid=(bsz, nb),
        in_specs=[pl.BlockSpec((qb, A_Q), lambda b, j: (b * nb + j, COL_QA // A_Q)),
                  pl.BlockSpec((qb, A_Q), lambda b, j: (b * nb + j, IDX_COL_QI // A_Q)),
                  pl.BlockSpec((qb, 128), lambda b, j: (b * nb + j, IDX_COL_MISC // 128)),
                  pl.BlockSpec((seq, 2 * A_KV), lambda b, j: (b, COL_KV // (2 * A_KV))),
                  pl.BlockSpec((seq, 128), lambda b, j: (b, IDX_COL_MISC // 128))],
        out_specs=pl.BlockSpec((qb, A_Q), lambda b, j: (b * nb + j, 0)),
        out_shape=jax.ShapeDtypeStruct((bsz * seq, A_Q), BF16),
        scratch_shapes=_attn_scratch(qb, seq),
        compiler_params=_params(("parallel", "arbitrary")),
        name="attn_prompt",
    )(proj, idxp, idxp, proj, idxp)


def _attention_sample(proj, idxp, kv_all, ik_all, bsz, tlen, row0, s_valid):
    s_pad = kv_all.shape[1]
    assert tlen % Q_BLOCK != 0 and row0 % tlen == 0
    assert min(TOPK_MAX, s_valid // 4) == TOPK_MAX
    rb0 = row0 // tlen
    kern = functools.partial(_attn_sample_kernel, qb=tlen, s_pad=s_pad, s_valid=s_valid,
                             q_start=s_valid - tlen)
    return pl.pallas_call(
        kern,
        grid=(bsz,),
        in_specs=[pl.BlockSpec((tlen, A_Q), lambda b: (rb0 + b, COL_QA // A_Q)),
                  pl.BlockSpec((tlen, A_Q), lambda b: (rb0 + b, IDX_COL_QI // A_Q)),
                  pl.BlockSpec((tlen, 128), lambda b: (rb0 + b, IDX_COL_MISC // 128)),
                  pl.BlockSpec((1, s_pad, 2 * A_KV), lambda b: (b, 0, 0)),
                  pl.BlockSpec((1, s_pad, 128), lambda b: (b, 0, 0))],
        out_specs=pl.BlockSpec((tlen, A_Q), lambda b: (b, 0)),
        out_shape=jax.ShapeDtypeStruct((bsz * tlen, A_Q), BF16),
        scratch_shapes=_attn_scratch(tlen, s_pad),
        compiler_params=_params(("parallel",)),
        name="attn_sample",
    )(proj, idxp, idxp, kv_all, ik_all)


def _gdn_pre_kernel(qc_ref, kc_ref, vc_ref, qp_ref, kp_ref, vp_ref, misc_ref, conv0_ref, convw_ref,
                    alog_ref, dtb_ref, u_ref, wq_ref, qk_ref, ktt_ref, egl_ref, ctail_ref, ext_scr, *,
                    c_len, cb, steps):
    i = pl.program_id(1)
    rows = cb * c_len

    @pl.when(i == 0)
    def _():
        ext_scr[0:8, :] = conv0_ref[0]

    @pl.when(i > 0)
    def _():
        ext_scr[0:8, 0:B_QK] = qp_ref[...]
        ext_scr[0:8, B_QK:2 * B_QK] = kp_ref[...]
        ext_scr[0:8, 2 * B_QK:B_CONV_CH] = vp_ref[...]

    ext_scr[8:8 + rows, 0:B_QK] = qc_ref[...]
    ext_scr[8:8 + rows, B_QK:2 * B_QK] = kc_ref[...]
    ext_scr[8:8 + rows, 2 * B_QK:B_CONV_CH] = vc_ref[...]
    w = convw_ref[...]
    y = None
    for j in range(CONV_WIDTH):
        off = 8 - (CONV_WIDTH - 1) + j
        term = ext_scr[off:off + rows, :] * w[j:j + 1, :]
        y = term if y is None else y + term
    y = _silu(y)

    @pl.when(i == steps - 1)
    def _():
        ctail_ref[0] = ext_scr[rows:rows + 8, :]

    misc = misc_ref[...]
    sp_in = misc + dtb_ref[...]
    softplus = jnp.maximum(sp_in, 0.0) + jnp.log1p(jnp.exp(-jnp.abs(sp_in)))
    g_tile = -jnp.exp(alog_ref[...]) * softplus
    beta_tile = _sigmoid(misc)

    rr = lax.broadcasted_iota(I32, (rows, rows), 0)
    cc = lax.broadcasted_iota(I32, (rows, rows), 1)
    c_shift = c_len.bit_length() - 1
    assert (1 << c_shift) == c_len
    same = (rr >> c_shift) == (cc >> c_shift)
    tril = jnp.where(same, jnp.where(rr >= cc, 1.0, 0.0), 0.0).astype(BF16)
    g_cum = None
    for piece in _split3(g_tile):
        part = _dot(tril, piece)
        g_cum = part if g_cum is None else g_cum + part
    e_cum = jnp.exp(g_cum)

    row = lax.broadcasted_iota(I32, (c_len, c_len), 0)
    col = lax.broadcasted_iota(I32, (c_len, c_len), 1)
    incl = row >= col
    strict = row > col
    lane_c = lax.broadcasted_iota(I32, (c_len, 128), 1)
    eye_k = _eye(B_KEY_DIM, B_KEY_DIM, BF16)

    chains = []
    for ci in range(cb):
        r0 = ci * c_len
        gcum_c = g_cum[r0:r0 + c_len, :]
        ecum_c = e_cum[r0:r0 + c_len, :]
        g_last = gcum_c[c_len - 1:c_len, :]
        tail_c = jnp.exp(g_last - gcum_c)
        egl_ref[ci] = jnp.broadcast_to(ecum_c[c_len - 1:c_len, :], (8, 128))
        gc_pieces = _split3(gcum_c)
        for h in range(B_HEADS):
            qh = y[r0:r0 + c_len, h * B_KEY_DIM:(h + 1) * B_KEY_DIM]
            kh = y[r0:r0 + c_len, B_QK + h * B_KEY_DIM:B_QK + (h + 1) * B_KEY_DIM]
            vh = y[r0:r0 + c_len, 2 * B_QK + h * B_VAL_DIM:2 * B_QK + (h + 1) * B_VAL_DIM]
            qh = qh * lax.rsqrt(jnp.sum(qh * qh, axis=-1, keepdims=True) + RMS_EPS)
            kh = kh * lax.rsqrt(jnp.sum(kh * kh, axis=-1, keepdims=True) + RMS_EPS)
            lane = MISC_A + h
            beta = beta_tile[r0:r0 + c_len, MISC_B + h:MISC_B + h + 1]
            gc = gcum_c[:, lane:lane + 1]
            eg = ecum_c[:, lane:lane + 1]
            pick = (lane_c == lane).astype(BF16)
            gr = None
            for piece in gc_pieces:
                part = _dot_nt(pick, piece)
                gr = part if gr is None else gr + part
            decay = jnp.where(incl, jnp.exp(jnp.where(incl, gc - gr, 0.0)), 0.0)
            kb = kh * beta
            m = jnp.where(strict, _mm1_nt(kb, kh) * decay, 0.0)
            rhs = jnp.concatenate([vh * beta, kb * eg], axis=1)
            qs = qh * (B_KEY_DIM ** -0.5)
            qk = jnp.where(incl, _mm1_nt(qs, kh) * decay, 0.0)
            kt = (kh * tail_c[:, lane:lane + 1]).astype(BF16)
            hs = slice(h * B_VAL_DIM, (h + 1) * B_VAL_DIM)
            wq_ref[ci, c_len:2 * c_len, hs] = (qs * eg).astype(BF16)
            qk_ref[ci, h] = qk.astype(BF16)
            ktt_ref[ci, h] = _dot_nt(eye_k, kt).astype(BF16)
            chains.append((ci, r0, hs, m, rhs))

    eye_c = _eye(c_len, c_len, F32)
    xs = [-ch[3] for ch in chains]
    ps = [eye_c + x for x in xs]
    n = 2
    while n < c_len:
        xs = [_mm3(x, x) for x in xs]
        ps = [p + _mm3(p, x) for p, x in zip(ps, xs)]
        n *= 2
    sols = [_mm3(p, ch[4]) for p, ch in zip(ps, chains)]
    for (ci, r0, hs, _, _), sol in zip(chains, sols):
        u_ref[r0:r0 + c_len, hs] = sol[:, 0:B_VAL_DIM]
        wq_ref[ci, 0:c_len, hs] = sol[:, B_VAL_DIM:].astype(BF16)


def _gdn_state_kernel(*refs, c_len, n_chunks, gb):
    u_ref, wq_ref, qk_ref, ktt_ref, egl_ref = refs[0:5]
    z_refs = refs[5:5 + gb]
    s0_ref, gn_ref, o_ref, s_out_ref, s_scr = refs[5 + gb:]
    c = pl.program_id(1)

    @pl.when(c == 0)
    def _():
        s_scr[...] = s0_ref[...]

    ids = [(g, h, slice(h * B_VAL_DIM, (h + 1) * B_VAL_DIM)) for g in range(gb) for h in range(B_HEADS)]
    a1s = [_dot(wq_ref[g, :, hs], s_scr[g, h].astype(BF16)) for g, h, hs in ids]
    vbs = [(u_ref[g, :, hs] - a1[0:c_len]).astype(BF16) for (g, h, hs), a1 in zip(ids, a1s)]
    upd = [_dot(ktt_ref[g, 0, h], vb) for (g, h, hs), vb in zip(ids, vbs)]
    for (g, h, hs), d in zip(ids, upd):
        s_scr[g, h] = s_scr[g, h] * egl_ref[g, 0, 0:1, MISC_A + h:MISC_A + h + 1] + d
    outs = [a1[c_len:2 * c_len] + _dot(qk_ref[g, 0, h], vb) for (g, h, hs), a1, vb in zip(ids, a1s, vbs)]
    for (g, h, hs), o in zip(ids, outs):
        o = o * lax.rsqrt(jnp.mean(o * o, axis=-1, keepdims=True) + RMS_EPS) * gn_ref[...]
        o = o * _silu(z_refs[g][:, hs])
        o_ref[g, :, hs] = o.astype(o_ref.dtype)

    @pl.when(c == n_chunks - 1)
    def _():
        s_out_ref[...] = s_scr[...]


def _gated_deltanet(proj, conv0_pad, s0, conv_w, alog_row, dtb_row, gnorm_row, bsz, tlen, row0):
    c_len = CHUNK if tlen % CHUNK == 0 else tlen
    n_chunks = tlen // c_len
    cb = 2 if n_chunks % 2 == 0 else 1
    rows = cb * c_len
    steps = n_chunks // cb
    assert row0 % rows == 0 and row0 % 8 == 0 and c_len >= 8
    rb0 = row0 // rows

    def cur(col_block):
        return pl.BlockSpec((rows, B_QK), lambda b, i: (rb0 + b * steps + i, col_block))

    def prev(col_block):
        return pl.BlockSpec((8, B_QK),
                            lambda b, i: (jnp.maximum((row0 + b * tlen + i * rows) // 8 - 1, 0), col_block))

    const2 = lambda shape: pl.BlockSpec(shape, lambda b, i: (0, 0))
    nck = bsz * n_chunks
    u, wq, qk, ktt, egl, ctail = pl.pallas_call(
        functools.partial(_gdn_pre_kernel, c_len=c_len, cb=cb, steps=steps),
        grid=(bsz, steps),
        in_specs=[cur(COL_QB // B_QK), cur(COL_KB // B_QK), cur(COL_VB // B_QK),
                  prev(COL_QB // B_QK), prev(COL_KB // B_QK), prev(COL_VB // B_QK),
                  pl.BlockSpec((rows, 128), lambda b, i: (rb0 + b * steps + i, COL_MISC // 128)),
                  pl.BlockSpec((1, 8, B_CONV_CH), lambda b, i: (b, 0, 0)),
                  const2((CONV_WIDTH, B_CONV_CH)), const2((1, 128)), const2((1, 128))],
        out_specs=[pl.BlockSpec((rows, B_V), lambda b, i: (b * steps + i, 0)),
                   pl.BlockSpec((cb, 2 * c_len, B_V), lambda b, i: (b * steps + i, 0, 0)),
                   pl.BlockSpec((cb, B_HEADS, c_len, c_len), lambda b, i: (b * steps + i, 0, 0, 0)),
                   pl.BlockSpec((cb, B_HEADS, B_KEY_DIM, c_len), lambda b, i: (b * steps + i, 0, 0, 0)),
                   pl.BlockSpec((cb, 8, 128), lambda b, i: (b * steps + i, 0, 0)),
                   pl.BlockSpec((1, 8, B_CONV_CH), lambda b, i: (b, 0, 0))],
        out_shape=[jax.ShapeDtypeStruct((bsz * tlen, B_V), F32),
                   jax.ShapeDtypeStruct((nck, 2 * c_len, B_V), BF16),
                   jax.ShapeDtypeStruct((nck, B_HEADS, c_len, c_len), BF16),
                   jax.ShapeDtypeStruct((nck, B_HEADS, B_KEY_DIM, c_len), BF16),
                   jax.ShapeDtypeStruct((nck, 8, 128), F32),
                   jax.ShapeDtypeStruct((bsz, 8, B_CONV_CH), F32)],
        scratch_shapes=[pltpu.VMEM((8 + rows, B_CONV_CH), F32)],
        compiler_params=_params(("parallel", "arbitrary")),
        name="gdn_pre",
    )(proj, proj, proj, proj, proj, proj, proj, conv0_pad, conv_w, alog_row, dtb_row)

    gb = 4 if bsz % 4 == 0 else 1
    zrb0 = row0 // c_len

    def z_spec(g):
        return pl.BlockSpec((c_len, B_V),
                            lambda bg, c, g=g: (zrb0 + (bg * gb + g) * n_chunks + c, COL_Z // B_V))

    o_b, s_new = pl.pallas_call(
        functools.partial(_gdn_state_kernel, c_len=c_len, n_chunks=n_chunks, gb=gb),
        grid=(bsz // gb, n_chunks),
        in_specs=[pl.BlockSpec((gb, c_len, B_V), lambda bg, c: (bg, c, 0)),
                  pl.BlockSpec((gb, 2 * c_len, B_V), lambda bg, c: (bg, c, 0)),
                  pl.BlockSpec((gb, 1, B_HEADS, c_len, c_len), lambda bg, c: (bg, c, 0, 0, 0)),
                  pl.BlockSpec((gb, 1, B_HEADS, B_KEY_DIM, c_len), lambda bg, c: (bg, c, 0, 0, 0)),
                  pl.BlockSpec((gb, 1, 8, 128), lambda bg, c: (bg, c, 0, 0))]
                 + [z_spec(g) for g in range(gb)]
                 + [pl.BlockSpec((gb, B_HEADS, B_KEY_DIM, B_VAL_DIM), lambda bg, c: (bg, 0, 0, 0)),
                    pl.BlockSpec((1, B_VAL_DIM), lambda bg, c: (0, 0))],
        out_specs=[pl.BlockSpec((gb, c_len, B_V), lambda bg, c: (bg, c, 0)),
                   pl.BlockSpec((gb, B_HEADS, B_KEY_DIM, B_VAL_DIM), lambda bg, c: (bg, 0, 0, 0))],
        out_shape=[jax.ShapeDtypeStruct((bsz, tlen, B_V), BF16),
                   jax.ShapeDtypeStruct((bsz, B_HEADS, B_KEY_DIM, B_VAL_DIM), F32)],
        scratch_shapes=[pltpu.VMEM((gb, B_HEADS, B_KEY_DIM, B_VAL_DIM), F32)],
        compiler_params=_params(("parallel", "arbitrary")),
        name="gdn_state",
    )(u.reshape(bsz, tlen, B_V), wq.reshape(bsz, n_chunks * 2 * c_len, B_V),
      qk.reshape(bsz, n_chunks, B_HEADS, c_len, c_len), ktt.reshape(bsz, n_chunks, B_HEADS, B_KEY_DIM, c_len),
      egl.reshape(bsz, n_chunks, 8, 128), *([proj] * gb), s0, gnorm_row)
    return o_b.reshape(bsz * tlen, B_V), s_new, ctail[:, 8 - (CONV_WIDTH - 1):, :]


def _post_kernel(*refs, aliased):
    oa_ref, ob_ref, ga_ref, gb_ref, h_ref, woa_ref, wob_ref, wout_ref, g_ref, b_ref = refs[0:10]
    o_ref, ob16_ref = refs[-2:]
    ya = _dot(oa_ref[...], woa_ref[...])
    yb = _dot(ob_ref[...], wob_ref[...])
    mix = _sigmoid(ga_ref[...]) * ya + _sigmoid(gb_ref[...]) * yb
    y = _dot(mix.astype(BF16), wout_ref[...])
    out = _ln(DEEPNORM_ALPHA * h_ref[...] + y, g_ref[...], b_ref[...])
    o_ref[...] = out
    ob16_ref[...] = out.astype(BF16)


def _post_mixer_stream(o_a, o_b, proj, h, weights, row0, bufs):
    w_oa, w_ob, w_out, ln_g, ln_b = weights
    n_rows = o_a.shape[0]
    n = h.shape[0]
    tm = _pick_tile(n_rows, (1024, 512, 256, 128, 64, 32, 16))
    assert row0 % tm == 0
    r0 = row0 // tm
    loc = lambda w: pl.BlockSpec((tm, w), lambda i: (i, 0))
    glob = lambda w, cb: pl.BlockSpec((tm, w), lambda i: (r0 + i, cb))
    full = lambda a, b: pl.BlockSpec((a, b), lambda i: (0, 0))
    in_specs = [loc(A_Q), loc(B_V), glob(D_MODEL, COL_GA // D_MODEL), glob(D_MODEL, COL_GB // D_MODEL),
                glob(D_MODEL, 0), full(A_Q, D_MODEL), full(B_V, D_MODEL), full(D_MODEL, D_MODEL),
                full(1, D_MODEL), full(1, D_MODEL)]
    args = [o_a, o_b, proj, proj, h, w_oa, w_ob, w_out, ln_g.reshape(1, D_MODEL), ln_b.reshape(1, D_MODEL)]
    aliases = {}
    if bufs is not None:
        in_specs += [_any_spec(), _any_spec()]
        args += list(bufs)
        aliases = {10: 0, 11: 1}
    return pl.pallas_call(
        functools.partial(_post_kernel, aliased=bufs is not None),
        grid=(n_rows // tm,),
        in_specs=in_specs,
        out_specs=[glob(D_MODEL, 0), glob(D_MODEL, 0)],
        out_shape=[jax.ShapeDtypeStruct((n, D_MODEL), F32), jax.ShapeDtypeStruct((n, D_MODEL), BF16)],
        input_output_aliases=aliases,
        compiler_params=_params(("parallel",)),
        name="post_mixer",
    )(*args)


def _ffn_kernel(h_ref, hb_ref, w1_ref, w3_ref, w2_ref, g_ref, b_ref, o_ref, acc_ref, *, n_ff):
    j = pl.program_id(1)
    hb = hb_ref[...]
    a = _dot(hb, w1_ref[...])
    b = _dot(hb, w3_ref[...])
    part = _dot((_silu(a) * b).astype(BF16), w2_ref[...])

    @pl.when(j == 0)
    def _():
        acc_ref[...] = part

    @pl.when(j > 0)
    def _():
        acc_ref[...] += part

    @pl.when(j == n_ff - 1)
    def _():
        o_ref[...] = _ln(DEEPNORM_ALPHA * h_ref[...] + acc_ref[...], g_ref[...], b_ref[...])


def _dense_ffn(h, hb, w1, w3, w2, ln_g, ln_b):
    n = h.shape[0]
    tm = _pick_tile(n, (768, 512, 256, 128, 64, 32, 16))
    tf = D_FF // 2
    n_ff = D_FF // tf
    return pl.pallas_call(
        functools.partial(_ffn_kernel, n_ff=n_ff),
        grid=(n // tm, n_ff),
        in_specs=[pl.BlockSpec((tm, D_MODEL), lambda i, j: (i, 0)),
                  pl.BlockSpec((tm, D_MODEL), lambda i, j: (i, 0)),
                  pl.BlockSpec((D_MODEL, tf), lambda i, j: (0, j)),
                  pl.BlockSpec((D_MODEL, tf), lambda i, j: (0, j)),
                  pl.BlockSpec((tf, D_MODEL), lambda i, j: (j, 0)),
                  pl.BlockSpec((1, D_MODEL), lambda i, j: (0, 0)),
                  pl.BlockSpec((1, D_MODEL), lambda i, j: (0, 0))],
        out_specs=pl.BlockSpec((tm, D_MODEL), lambda i, j: (i, 0)),
        out_shape=jax.ShapeDtypeStruct((n, D_MODEL), F32),
        scratch_shapes=[pltpu.VMEM((tm, D_MODEL), F32)],
        compiler_params=_params(("parallel", "arbitrary")),
        name="dense_ffn",
    )(h, hb, w1, w3, w2, ln_g.reshape(1, D_MODEL), ln_b.reshape(1, D_MODEL))


def _router_kernel(h_ref, r_ref, gate_ref):
    logits = _mm3(h_ref[...], r_ref[...])
    lane = lax.broadcasted_iota(I32, logits.shape, 1)
    neg_inf = jnp.float32(-jnp.inf)
    lg = jnp.where(lane < N_EXPERTS, logits, neg_inf)
    v1 = jnp.max(lg, axis=-1, keepdims=True)
    i1 = jnp.min(jnp.where(lg == v1, lane, 128), axis=-1, keepdims=True)
    lg2 = jnp.where(lane == i1, neg_inf, lg)
    v2 = jnp.max(lg2, axis=-1, keepdims=True)
    i2 = jnp.min(jnp.where(lg2 == v2, lane, 128), axis=-1, keepdims=True)
    e2 = jnp.exp(v2 - v1)
    w1 = 1.0 / (1.0 + e2)
    w2 = e2 / (1.0 + e2)
    gate_ref[...] = jnp.where(lane == i1, w1, jnp.where(lane == i2, w2, 0.0))


def _router(h, router_pad):
    n = h.shape[0]
    tm = _pick_tile(n, (768, 512, 256, 128, 64, 32, 16, 8))
    return pl.pallas_call(
        _router_kernel,
        grid=(n // tm,),
        in_specs=[pl.BlockSpec((tm, D_MODEL), lambda i: (i, 0)),
                  pl.BlockSpec((D_MODEL, 128), lambda i: (0, 0))],
        out_specs=pl.BlockSpec((tm, 128), lambda i: (i, 0)),
        out_shape=jax.ShapeDtypeStruct((n, 128), F32),
        compiler_params=_params(("parallel",)),
        name="moe_router",
    )(h, router_pad)


def _expert_kernel(te_ref, nt_ref, x_ref, w1_ref, w3_ref, w2_ref, o_ref, acc_ref, *, n_ff):
    i = pl.program_id(0)
    j = pl.program_id(1)

    @pl.when(i < nt_ref[0])
    def _():
        x = x_ref[...]
        a = _dot(x, w1_ref[0])
        b = _dot(x, w3_ref[0])
        part = _dot((_silu(a) * b).astype(BF16), w2_ref[0])

        @pl.when(j == 0)
        def _():
            acc_ref[...] = part

        @pl.when(j > 0)
        def _():
            acc_ref[...] += part

        @pl.when(j == n_ff - 1)
        def _():
            o_ref[...] = acc_ref[...]


def _grouped_experts(x_sorted, tile_expert, n_used, w1, w3, w2, tm):
    n_tiles = x_sorted.shape[0] // tm
    tf = D_FF_EXPERT // 2
    n_ff = D_FF_EXPERT // tf

    def row_map(i, j, te, nt):
        return (jnp.minimum(i, nt[0] - 1), 0)

    def ff_of(i, j, nt):
        return jnp.where(i < nt[0], j, n_ff - 1)

    grid_spec = pltpu.PrefetchScalarGridSpec(
        num_scalar_prefetch=2,
        grid=(n_tiles, n_ff),
        in_specs=[pl.BlockSpec((tm, D_MODEL), row_map),
                  pl.BlockSpec((1, D_MODEL, tf), lambda i, j, te, nt: (te[i], 0, ff_of(i, j, nt))),
                  pl.BlockSpec((1, D_MODEL, tf), lambda i, j, te, nt: (te[i], 0, ff_of(i, j, nt))),
                  pl.BlockSpec((1, tf, D_MODEL), lambda i, j, te, nt: (te[i], ff_of(i, j, nt), 0))],
        out_specs=pl.BlockSpec((tm, D_MODEL), row_map),
        scratch_shapes=[pltpu.VMEM((tm, D_MODEL), F32)],
    )
    return pl.pallas_call(
        functools.partial(_expert_kernel, n_ff=n_ff),
        grid_spec=grid_spec,
        out_shape=jax.ShapeDtypeStruct((n_tiles * tm, D_MODEL), F32),
        compiler_params=_params(("arbitrary", "arbitrary")),
        name="moe_experts",
    )(tile_expert, n_used, x_sorted, w1, w3, w2)


def _combine_kernel(h_ref, y1_ref, y2_ref, g_ref, b_ref, o_ref):
    f = y1_ref[...] + y2_ref[...]
    o_ref[...] = _ln(DEEPNORM_ALPHA * h_ref[...] + f, g_ref[...], b_ref[...])


def _combine_ln(h, y1, y2, ln_g, ln_b):
    n = h.shape[0]
    tm = _pick_tile(n, (768, 512, 256, 128, 64, 32, 16, 8))
    row = pl.BlockSpec((tm, D_MODEL), lambda i: (i, 0))
    vec = pl.BlockSpec((1, D_MODEL), lambda i: (0, 0))
    return pl.pallas_call(
        _combine_kernel,
        grid=(n // tm,),
        in_specs=[row, row, row, vec, vec],
        out_specs=row,
        out_shape=jax.ShapeDtypeStruct((n, D_MODEL), F32),
        compiler_params=_params(("parallel",)),
        name="moe_combine",
    )(h, y1, y2, ln_g.reshape(1, D_MODEL), ln_b.reshape(1, D_MODEL))


def _moe_ffn(h, hb, router, w1, w3, w2, ln_g, ln_b):
    n = h.shape[0]
    router_pad = jnp.zeros((D_MODEL, 128), F32).at[:, :N_EXPERTS].set(router)
    gate = _router(h, router_pad)[:, :N_EXPERTS]
    tm = 512 if n >= 4096 else 64
    top_w, top_e = lax.top_k(gate, 2)
    flat_e = top_e.reshape(-1).astype(I32)
    tok = jnp.arange(2 * n, dtype=I32) // 2
    onehot = (flat_e[:, None] == jnp.arange(N_EXPERTS, dtype=I32)[None, :]).astype(I32)
    csum = jnp.cumsum(onehot, axis=0)
    rank = jnp.sum(onehot * csum, axis=1) - 1
    counts = csum[-1]
    padded = ((counts + tm - 1) // tm) * tm
    starts = jnp.cumsum(counts) - counts
    pend = jnp.cumsum(padded)
    pstarts = pend - padded
    row_of = jnp.sum(onehot * pstarts[None, :], axis=1) + rank
    n_tiles = (2 * n) // tm + N_EXPERTS
    tile_start = jnp.arange(n_tiles, dtype=I32) * tm
    n_used = (pend[-1] // tm).astype(I32).reshape(1)
    last_start = jnp.minimum(tile_start, (n_used[0] - 1) * tm)
    tile_expert = jnp.sum((last_start[:, None] >= pend[None, :]).astype(I32), axis=1).astype(I32)
    _, sorted_tok = lax.sort((flat_e, tok), num_keys=1, is_stable=True)
    r = jnp.arange(n_tiles * tm, dtype=I32)
    e_r = jnp.repeat(tile_expert, tm)
    src_idx = jnp.clip(starts[e_r] + (r - pstarts[e_r]), 0, 2 * n - 1)
    src_tok = sorted_tok[src_idx]
    x_sorted = jnp.take(hb, src_tok, axis=0)
    y_sorted = _grouped_experts(x_sorted, tile_expert, n_used, w1, w3, w2, tm)
    rows = row_of.reshape(n, 2)
    y1 = jnp.take(y_sorted, rows[:, 0], axis=0) * top_w[:, 0:1]
    y2 = jnp.take(y_sorted, rows[:, 1], axis=0) * top_w[:, 1:2]
    return _combine_ln(h, y1, y2, ln_g, ln_b)


def kernel(x_prompt, x_sample, cache_k, cache_v, cache_idx_k, state_conv, state_gdn,
           ln_in_g, ln_in_b, w_in, conv_w, a_log, dt_bias, gnorm_g, w_oa, w_ob, w_out,
           ln1_g, ln1_b, ln2_g, ln2_b, ffn_w1, ffn_w3, ffn_w2,
           moe_router, moe_w1, moe_w3, moe_w2):
    bp, tp, _ = x_prompt.shape
    bs, ts, _ = x_sample.shape
    past = cache_k.shape[2]
    n_p = bp * tp
    n_s = bs * ts
    h = _layer_norm_streams(x_prompt.reshape(n_p, D_MODEL), x_sample.reshape(n_s, D_MODEL), ln_in_g, ln_in_b)

    s_valid = past + ts
    s_pad = ((s_valid + 127) // 128) * 128
    lane_pad = jnp.zeros((1, 128), F32)
    conv0_p = jnp.zeros((bp, 8, B_CONV_CH), F32)
    s0_p = jnp.zeros((bp, B_HEADS, B_KEY_DIM, B_VAL_DIM), F32)

    outs = {k: [] for k in ("kp", "vp", "ikp", "convp", "sp", "ks", "vs", "iks", "convs", "ss")}
    for l in range(DEPTH):
        w_packed, w_idx_hi, w_idx_lo = _pack_w_in(w_in[l])
        proj, k_new, v_new = _project(h, w_packed)
        idxp, ik_new = _project_idx(h, w_idx_hi, w_idx_lo)

        oa_p = _attention_prompt(proj, idxp, bp, tp)
        kv_s = jnp.concatenate([k_new[n_p:], v_new[n_p:]], axis=-1).reshape(bs, ts, 2 * A_KV)
        kv_cache = jnp.concatenate([cache_k[l].reshape(bs, past, A_KV), cache_v[l].reshape(bs, past, A_KV)], axis=-1)
        kv_all = jnp.concatenate([kv_cache, kv_s, jnp.zeros((bs, s_pad - s_valid, 2 * A_KV), F32)], axis=1)
        ik_s = jnp.pad(ik_new[n_p:].reshape(bs, ts, IDX_DIM), ((0, 0), (0, 0), (0, 128 - IDX_DIM)))
        ik_cache = jnp.pad(cache_idx_k[l], ((0, 0), (0, 0), (0, 128 - IDX_DIM)))
        ik_all = jnp.concatenate([ik_cache, ik_s, jnp.zeros((bs, s_pad - s_valid, 128), F32)], axis=1)
        oa_s = _attention_sample(proj, idxp, kv_all, ik_all, bs, ts, n_p, s_valid)

        alog_row = lax.dynamic_update_slice(lane_pad, a_log[l].reshape(1, B_HEADS), (0, MISC_A))
        dtb_row = lax.dynamic_update_slice(lane_pad, dt_bias[l].reshape(1, B_HEADS), (0, MISC_A))
        gn_row = gnorm_g[l].reshape(1, B_VAL_DIM)
        conv0_s = jnp.pad(state_conv[l], ((0, 0), (8 - (CONV_WIDTH - 1), 0), (0, 0)))
        ob_p, s_p, qkv_p = _gated_deltanet(proj, conv0_p, s0_p, conv_w[l], alog_row, dtb_row, gn_row, bp, tp, 0)
        ob_s, s_s, qkv_s = _gated_deltanet(proj, conv0_s, state_gdn[l], conv_w[l], alog_row, dtb_row, gn_row,
                                           bs, ts, n_p)

        weights = (w_oa[l].astype(BF16), w_ob[l].astype(BF16), w_out[l].astype(BF16), ln1_g[l], ln1_b[l])
        bufs = _post_mixer_stream(oa_p, ob_p, proj, h, weights, 0, None)
        h1, h1b = _post_mixer_stream(oa_s, ob_s, proj, h, weights, n_p, bufs)
        i = l // 2
        if l % 2 == 0:
            h = _dense_ffn(h1, h1b, ffn_w1[i].astype(BF16), ffn_w3[i].astype(BF16), ffn_w2[i].astype(BF16),
                           ln2_g[l], ln2_b[l])
        else:
            h = _moe_ffn(h1, h1b, moe_router[i], moe_w1[i].astype(BF16), moe_w3[i].astype(BF16),
                         moe_w2[i].astype(BF16), ln2_g[l], ln2_b[l])

        outs["kp"].append(k_new[:n_p].reshape(bp, tp, A_KV_HEADS, A_HEAD_DIM))
        outs["vp"].append(v_new[:n_p].reshape(bp, tp, A_KV_HEADS, A_HEAD_DIM))
        outs["ikp"].append(ik_new[:n_p].reshape(bp, tp, IDX_DIM))
        outs["convp"].append(qkv_p)
        outs["sp"].append(s_p)
        outs["ks"].append(k_new[n_p:].reshape(bs, ts, A_KV_HEADS, A_HEAD_DIM))
        outs["vs"].append(v_new[n_p:].reshape(bs, ts, A_KV_HEADS, A_HEAD_DIM))
        outs["iks"].append(ik_new[n_p:].reshape(bs, ts, IDX_DIM))
        outs["convs"].append(qkv_s)
        outs["ss"].append(s_s)

    st = lambda name: jnp.stack(outs[name])
    return (h[:n_p].reshape(bp, tp, D_MODEL), h[n_p:].reshape(bs, ts, D_MODEL),
            st("kp"), st("vp"), st("ikp"), st("convp"), st("sp"),
            st("ks"), st("vs"), st("iks"), st("convs"), st("ss"))
```

```python
import functools

import jax
import jax.numpy as jnp
from jax import lax
from jax.experimental import pallas as pl
from jax.experimental.pallas import tpu as pltpu

F32 = jnp.float32
BF16 = jnp.bfloat16
I32 = jnp.int32

D_MODEL = 1024
DEPTH = 2
CHUNK = 64
Q_BLOCK = 128
A_HEADS = 8
A_KV_HEADS = 2
A_HEAD_DIM = 64
IDX_HEADS = 8
IDX_DIM = 64
TOPK_MAX = 256
B_HEADS = 4
B_KEY_DIM = 128
B_VAL_DIM = 128
CONV_WIDTH = 4
D_FF = 2816
N_EXPERTS = 8
D_FF_EXPERT = 3584
LN_EPS = 1e-5
RMS_EPS = 1e-6
DEEPNORM_ALPHA = (2 * DEPTH) ** 0.25
ALIBI_SLOPES = tuple(2.0 ** (-8.0 * (h + 1) / A_HEADS) for h in range(A_HEADS))

A_Q = A_HEADS * A_HEAD_DIM
A_KV = A_KV_HEADS * A_HEAD_DIM
A_GROUP = A_HEADS // A_KV_HEADS
B_QK = B_HEADS * B_KEY_DIM
B_V = B_HEADS * B_VAL_DIM
B_CONV_CH = 2 * B_QK + B_V

COL_GA = 0
COL_GB = 1024
COL_QB = 2048
COL_KB = 2560
COL_VB = 3072
COL_Z = 3584
COL_QA = 4096
COL_KV = 4608
COL_MISC = 4864
PROJ_W = 5120
PROJ_TN = 1280
IDX_COL_QI = 0
IDX_COL_MISC = IDX_HEADS * IDX_DIM
IDX_W = IDX_COL_MISC + 128
MISC_KI = 0
MISC_WI = 64
MISC_A = 72
MISC_B = 76
IDX_K_AUG = 4 * IDX_DIM

INT_MIN = -(2 ** 31)
VMEM_LIMIT = 56 * 1024 * 1024
VT_ROWS = 80


def _pick_tile(n, candidates):
    for c in candidates:
        if n % c == 0:
            return c
    raise ValueError(f"no tile for {n}")


def _params(sem):
    return pltpu.CompilerParams(dimension_semantics=sem, vmem_limit_bytes=VMEM_LIMIT)


def _ln(x, g, b):
    mu = jnp.mean(x, axis=-1, keepdims=True)
    xc = x - mu
    var = jnp.mean(xc * xc, axis=-1, keepdims=True)
    return xc * lax.rsqrt(var + LN_EPS) * g + b


def _dot(a, b):
    return jnp.dot(a, b, preferred_element_type=F32)


def _dot_nt(a, b):
    return lax.dot_general(a, b, (((1,), (1,)), ((), ())), preferred_element_type=F32)


def _mm1(a, b):
    return _dot(a.astype(BF16), b.astype(BF16))


def _mm1_nt(a, b):
    return _dot_nt(a.astype(BF16), b.astype(BF16))


def _split2(x):
    hi = x.astype(BF16)
    lo = (x - hi.astype(F32)).astype(BF16)
    return hi, lo


def _split3(x):
    hi = x.astype(BF16)
    r1 = x - hi.astype(F32)
    mid = r1.astype(BF16)
    lo = (r1 - mid.astype(F32)).astype(BF16)
    return hi, mid, lo


def _mm3(a, b):
    ah, al = _split2(a)
    bh, bl = _split2(b)
    return _dot(ah, bh) + (_dot(ah, bl) + _dot(al, bh))


def _eye(n, m, dtype):
    return (lax.broadcasted_iota(I32, (n, m), 0) == lax.broadcasted_iota(I32, (n, m), 1)).astype(dtype)


def _silu(x):
    return x * (1.0 / (1.0 + jnp.exp(-x)))


def _sigmoid(x):
    return 1.0 / (1.0 + jnp.exp(-x))


def _any_spec():
    return pl.BlockSpec(memory_space=pl.ANY)


def _ln_kernel(x_ref, g_ref, b_ref, o_ref):
    o_ref[...] = _ln(x_ref[...], g_ref[...], b_ref[...])


def _ln_alias_kernel(x_ref, g_ref, b_ref, buf_ref, o_ref):
    del buf_ref
    o_ref[...] = _ln(x_ref[...], g_ref[...], b_ref[...])


def _layer_norm_streams(x_p, x_s, g, b):
    n_p, n_s = x_p.shape[0], x_s.shape[0]
    n = n_p + n_s
    tm = _pick_tile(n_p, (1024, 512, 256, 128, 64, 32, 16, 8))
    assert n_p % n_s == 0
    vec = lambda: pl.BlockSpec((1, D_MODEL), lambda i: (0, 0))
    g2, b2 = g.reshape(1, D_MODEL), b.reshape(1, D_MODEL)
    h = pl.pallas_call(
        _ln_kernel,
        grid=(n_p // tm,),
        in_specs=[pl.BlockSpec((tm, D_MODEL), lambda i: (i, 0)), vec(), vec()],
        out_specs=pl.BlockSpec((tm, D_MODEL), lambda i: (i, 0)),
        out_shape=jax.ShapeDtypeStruct((n, D_MODEL), F32),
        compiler_params=_params(("parallel",)),
        name="ln_in_prompt",
    )(x_p, g2, b2)
    return pl.pallas_call(
        _ln_alias_kernel,
        grid=(1,),
        in_specs=[pl.BlockSpec((n_s, D_MODEL), lambda i: (0, 0)), vec(), vec(), _any_spec()],
        out_specs=pl.BlockSpec((n_s, D_MODEL), lambda i: (n_p // n_s, 0)),
        out_shape=jax.ShapeDtypeStruct((n, D_MODEL), F32),
        input_output_aliases={3: 0},
        compiler_params=_params(("arbitrary",)),
        name="ln_in_sample",
    )(x_s, g2, b2, h)


def _proj_kernel(h_ref, w_ref, o_ref, k_ref, v_ref, *, kv_tile):
    res = _dot(h_ref[...].astype(BF16), w_ref[...])
    o_ref[...] = res

    @pl.when(pl.program_id(1) == kv_tile)
    def _():
        c0 = COL_KV - kv_tile * PROJ_TN
        k_ref[...] = res[:, c0:c0 + A_KV]
        v_ref[...] = res[:, c0 + A_KV:c0 + 2 * A_KV]


def _proj_idx_kernel(h_ref, wh_ref, wl_ref, o_ref, ik_ref):
    hh, hl = _split2(h_ref[...])
    res = _dot(hh, wh_ref[...]) + (_dot(hh, wl_ref[...]) + _dot(hl, wh_ref[...]))
    o_ref[...] = res
    ik_ref[...] = res[:, IDX_COL_MISC + MISC_KI:IDX_COL_MISC + MISC_KI + IDX_DIM]


def _project_idx(h, w_hi, w_lo):
    n = h.shape[0]
    tm = _pick_tile(n, (768, 512, 256, 128, 64, 32, 16, 8))
    return pl.pallas_call(
        _proj_idx_kernel,
        grid=(n // tm,),
        in_specs=[pl.BlockSpec((tm, D_MODEL), lambda i: (i, 0)),
                  pl.BlockSpec((D_MODEL, IDX_W), lambda i: (0, 0)),
                  pl.BlockSpec((D_MODEL, IDX_W), lambda i: (0, 0))],
        out_specs=[pl.BlockSpec((tm, IDX_W), lambda i: (i, 0)),
                   pl.BlockSpec((tm, IDX_DIM), lambda i: (i, 0))],
        out_shape=[jax.ShapeDtypeStruct((n, IDX_W), F32),
                   jax.ShapeDtypeStruct((n, IDX_DIM), F32)],
        compiler_params=_params(("parallel",)),
        name="idx_proj",
    )(h, w_hi, w_lo)


def _project(h, w_packed):
    n = h.shape[0]
    tm = _pick_tile(n, (768, 512, 256, 128, 64, 32, 16, 8))
    kv_tile = COL_KV // PROJ_TN
    assert (COL_KV + 2 * A_KV - 1) // PROJ_TN == kv_tile
    return pl.pallas_call(
        functools.partial(_proj_kernel, kv_tile=kv_tile),
        grid=(n // tm, PROJ_W // PROJ_TN),
        in_specs=[pl.BlockSpec((tm, D_MODEL), lambda i, j: (i, 0)),
                  pl.BlockSpec((D_MODEL, PROJ_TN), lambda i, j: (0, j))],
        out_specs=[pl.BlockSpec((tm, PROJ_TN), lambda i, j: (i, j)),
                   pl.BlockSpec((tm, A_KV), lambda i, j: (i, 0)),
                   pl.BlockSpec((tm, A_KV), lambda i, j: (i, 0))],
        out_shape=[jax.ShapeDtypeStruct((n, PROJ_W), F32),
                   jax.ShapeDtypeStruct((n, A_KV), F32),
                   jax.ShapeDtypeStruct((n, A_KV), F32)],
        compiler_params=_params(("parallel", "arbitrary")),
        name="in_proj",
    )(h, w_packed)


def _pack_w_in(w):
    widths = (A_Q, A_KV, A_KV, IDX_HEADS * IDX_DIM, IDX_DIM, IDX_HEADS,
              B_CONV_CH, B_HEADS, B_HEADS, B_V, 2 * D_MODEL)
    offs = [0]
    for wd in widths:
        offs.append(offs[-1] + wd)
    qa, ka, va, qi, ki, wi, qkvb, ab, bb, zb, gates = [w[:, offs[i]:offs[i + 1]] for i in range(11)]
    zeros = lambda k: jnp.zeros((D_MODEL, k), w.dtype)
    packed = jnp.concatenate([gates, qkvb, zb, qa, ka, va, zeros(MISC_A), ab, bb,
                              zeros(128 - MISC_B - B_HEADS), zeros(PROJ_W - COL_MISC - 128)], axis=1)
    idx = jnp.concatenate([qi, ki, wi, zeros(128 - MISC_WI - IDX_HEADS)], axis=1)
    idx_hi = idx.astype(BF16)
    idx_lo = (idx - idx_hi.astype(F32)).astype(BF16)
    return packed.astype(BF16), idx_hi, idx_lo


def _col_reduce(get, s_len, qb, reduce_fn, pair_fn, n_acc=8):
    while s_len % (8 * n_acc) != 0:
        n_acc //= 2
    blk = s_len // n_acc
    parts = [reduce_fn(get(a * blk, (a + 1) * blk).reshape(blk // 8, 8, qb), axis=0) for a in range(n_acc)]
    while len(parts) > 1:
        parts = [pair_fn(parts[k], parts[k + 1]) for k in range(0, len(parts), 2)]
    return reduce_fn(parts[0], axis=0, keepdims=True)


def _attn_prep(kv_ref, ik_ref, ikb_scr, kg_scr, vt_scr, s_len):
    ik_hi, ik_lo = _split2(ik_ref[0:s_len, MISC_KI:MISC_KI + IDX_DIM])
    ikb_scr[0:s_len, :] = jnp.concatenate([ik_hi, ik_lo, ik_hi, jnp.zeros_like(ik_hi)], axis=1)
    pick = _eye(VT_ROWS, A_HEAD_DIM, BF16)
    ones_row = (lax.broadcasted_iota(I32, (VT_ROWS, s_len), 0) == A_HEAD_DIM).astype(F32)
    for g in range(A_KV_HEADS):
        kg_scr[g, 0:s_len, :] = kv_ref[0:s_len, g * A_HEAD_DIM:(g + 1) * A_HEAD_DIM].astype(BF16)
        v_g = kv_ref[0:s_len, A_KV + g * A_HEAD_DIM:A_KV + (g + 1) * A_HEAD_DIM].astype(BF16)
        vt_scr[g, :, 0:s_len] = (_dot_nt(pick, v_g) + ones_row).astype(BF16)


def _attn_core(q_ref, qi_ref, wm_ref, ikb_scr, kg_scr, vt_scr, o_ref, key_scr, bias_scr, *,
               qb, s_len, s_valid, t0):
    n_sel = float(TOPK_MAX)
    w_t = None
    for piece in _split3(wm_ref[...]):
        part = _dot_nt(_eye(128, 128, BF16), piece)
        w_t = part if w_t is None else w_t + part
    qi = qi_ref[...]
    ik = ikb_scr[0:s_len, :]
    score = None
    for h in range(IDX_HEADS):
        q_hi, q_lo = _split2(qi[:, h * IDX_DIM:(h + 1) * IDX_DIM])
        s = _dot_nt(ik, jnp.concatenate([q_hi, q_hi, q_lo, jnp.zeros_like(q_hi)], axis=1))
        term = jnp.maximum(s, 0.0) * w_t[MISC_WI + h:MISC_WI + h + 1, :]
        score = term if score is None else score + term
    score = score * (IDX_DIM ** -0.5)

    k_pos = lax.broadcasted_iota(I32, (s_len, qb), 0)
    t_pos = t0 + lax.broadcasted_iota(I32, (s_len, qb), 1)
    adm = (k_pos >> 6) <= (t_pos >> 6)
    if s_valid < s_len:
        adm = adm & (k_pos < s_valid)
    bits = pltpu.bitcast(score, I32)
    key = jnp.where(bits < 0, bits ^ jnp.int32(0x7FFFFFFF), bits)
    key = jnp.where(score == 0.0, jnp.int32(0), key)
    key_scr[0:s_len, :] = jnp.where(adm, key, jnp.int32(INT_MIN))

    def count_ge(thr):
        return _col_reduce(lambda lo, hi: jnp.where(key_scr[lo:hi, :] >= thr, 1.0, 0.0), s_len, qb, jnp.sum, jnp.add)

    def bit_step(i, t):
        cand = t ^ lax.shift_left(jnp.int32(1), jnp.int32(31) - i)
        return jnp.where(count_ge(cand) >= n_sel, cand, t)

    t = lax.fori_loop(0, 32, bit_step, jnp.full((1, qb), INT_MIN, I32))
    n_ge = count_ge(t)
    has_tie = jnp.where((n_ge > n_sel) & (t > jnp.int32(INT_MIN)), 1.0, 0.0)
    any_tie = jnp.max(has_tie) > 0.0
    neg_inf = jnp.float32(-jnp.inf)

    @pl.when(jnp.logical_not(any_tie))
    def _():
        t_eff = jnp.maximum(t, jnp.int32(INT_MIN + 1))
        bias_scr[0:s_len, :] = jnp.where(key_scr[0:s_len, :] >= t_eff, 0.0, neg_inf)

    @pl.when(any_tie)
    def _():
        kd = key_scr[0:s_len, :]
        kp = lax.broadcasted_iota(I32, (s_len, qb), 0)
        n_gt = jnp.sum(jnp.where(kd > t, 1.0, 0.0), axis=0, keepdims=True)
        need = n_sel - n_gt
        eq = kd == t

        def pos_step(i, p):
            cand = p + lax.shift_left(jnp.int32(1), jnp.int32(12) - i)
            cnt = jnp.sum(jnp.where(eq, jnp.where(kp < cand, 1.0, 0.0), 0.0), axis=0, keepdims=True)
            return jnp.where(cnt <= need, cand, p)

        p = lax.fori_loop(0, 13, pos_step, jnp.zeros((1, qb), I32))
        tie_bias = jnp.where(eq, jnp.where(kp < p, 0.0, neg_inf), neg_inf)
        sel_bias = jnp.where(kd > t, 0.0, tie_bias)
        bias_scr[0:s_len, :] = jnp.where(kd == jnp.int32(INT_MIN), neg_inf, sel_bias)

    dist = jnp.abs(t_pos - k_pos).astype(F32)
    q = q_ref[...]
    o_parts = []
    for g in range(A_KV_HEADS):
        for r in range(A_GROUP):
            h = g * A_GROUP + r
            qh = (q[:, h * A_HEAD_DIM:(h + 1) * A_HEAD_DIM] * (A_HEAD_DIM ** -0.5)).astype(BF16)
            lg = _dot_nt(kg_scr[g, 0:s_len, :], qh) - ALIBI_SLOPES[h] * dist + bias_scr[0:s_len, :]
            m = _col_reduce(lambda lo, hi: lg[lo:hi, :], s_len, qb, jnp.max, jnp.maximum)
            p = jnp.exp(lg - m).astype(BF16)
            ov = _dot(vt_scr[g, :, 0:s_len], p)
            o_parts.append(ov[0:A_HEAD_DIM] * (1.0 / ov[A_HEAD_DIM:A_HEAD_DIM + 1]))
    o_t = jnp.concatenate(o_parts, axis=0).astype(BF16)
    o_ref[...] = _dot_nt(_eye(qb, qb, BF16), o_t).astype(o_ref.dtype)


def _attn_prompt_kernel(q_ref, qi_ref, wm_ref, kv_ref, ik_ref, o_ref,
                        ikb_scr, kg_scr, vt_scr, key_scr, bias_scr, *, qb, seq):
    j = pl.program_id(1)

    @pl.when(j == 0)
    def _():
        _attn_prep(kv_ref, ik_ref, ikb_scr, kg_scr, vt_scr, seq)

    n_cls = 4 if seq % (4 * qb) == 0 and seq // qb >= 4 else 1
    step = seq // n_cls
    blocks_per_cls = step // qb
    for c in range(n_cls):
        @pl.when((j >= c * blocks_per_cls) & (j < (c + 1) * blocks_per_cls))
        def _(c=c):
            _attn_core(q_ref, qi_ref, wm_ref, ikb_scr, kg_scr, vt_scr, o_ref, key_scr, bias_scr,
                       qb=qb, s_len=(c + 1) * step, s_valid=(c + 1) * step, t0=j * qb)


def _attn_sample_kernel(q_ref, qi_ref, wm_ref, kv_ref, ik_ref, o_ref,
                        ikb_scr, kg_scr, vt_scr, key_scr, bias_scr, *, qb, s_pad, s_valid, q_start):
    _attn_prep(kv_ref.at[0], ik_ref.at[0], ikb_scr, kg_scr, vt_scr, s_pad)
    _attn_core(q_ref, qi_ref, wm_ref, ikb_scr, kg_scr, vt_scr, o_ref, key_scr, bias_scr,
               qb=qb, s_len=s_pad, s_valid=s_valid, t0=q_start)


def _attn_scratch(qb, s):
    return [pltpu.VMEM((s, IDX_K_AUG), BF16), pltpu.VMEM((A_KV_HEADS, s, A_HEAD_DIM), BF16),
            pltpu.VMEM((A_KV_HEADS, VT_ROWS, s), BF16), pltpu.VMEM((s, qb), I32), pltpu.VMEM((s, qb), F32)]


def _attention_prompt(proj, idxp, bsz, seq):
    qb = Q_BLOCK if seq % Q_BLOCK == 0 else seq
    nb = seq // qb
    assert min(TOPK_MAX, seq // 4) == TOPK_MAX
    kern = functools.partial(_attn_prompt_kernel, qb=qb, seq=seq)
    return pl.pallas_call(
        kern,
        grid=(bsz, nb),
        in_specs=[pl.BlockSpec((qb, A_Q), lambda b, j: (b * nb + j, COL_QA // A_Q)),
                  pl.BlockSpec((qb, A_Q), lambda b, j: (b * nb + j, IDX_COL_QI // A_Q)),
                  pl.BlockSpec((qb, 128), lambda b, j: (b * nb + j, IDX_COL_MISC // 128)),
                  pl.BlockSpec((seq, 2 * A_KV), lambda b, j: (b, COL_KV // (2 * A_KV))),
                  pl.BlockSpec((seq, 128), lambda b, j: (b, IDX_COL_MISC // 128))],
        out_specs=pl.BlockSpec((qb, A_Q), lambda b, j: (b * nb + j, 0)),
        out_shape=jax.ShapeDtypeStruct((bsz * seq, A_Q), BF16),
        scratch_shapes=_attn_scratch(qb, seq),
        compiler_params=_params(("parallel", "arbitrary")),
        name="attn_prompt",
    )(proj, idxp, idxp, proj, idxp)


def _attention_sample(proj, idxp, kv_all, ik_all, bsz, tlen, row0, s_valid):
    s_pad = kv_all.shape[1]
    assert tlen % Q_BLOCK != 0 and row0 % tlen == 0
    assert min(TOPK_MAX, s_valid // 4) == TOPK_MAX
    rb0 = row0 // tlen
    kern = functools.partial(_attn_sample_kernel, qb=tlen, s_pad=s_pad, s_valid=s_valid,
                             q_start=s_valid - tlen)
    return pl.pallas_call(
        kern,
        grid=(bsz,),
        in_specs=[pl.BlockSpec((tlen, A_Q), lambda b: (rb0 + b, COL_QA // A_Q)),
                  pl.BlockSpec((tlen, A_Q), lambda b: (rb0 + b, IDX_COL_QI // A_Q)),
                  pl.BlockSpec((tlen, 128), lambda b: (rb0 + b, IDX_COL_MISC // 128)),
                  pl.BlockSpec((1, s_pad, 2 * A_KV), lambda b: (b, 0, 0)),
                  pl.BlockSpec((1, s_pad, 128), lambda b: (b, 0, 0))],
        out_specs=pl.BlockSpec((tlen, A_Q), lambda b: (b, 0)),
        out_shape=jax.ShapeDtypeStruct((bsz * tlen, A_Q), BF16),
        scratch_shapes=_attn_scratch(tlen, s_pad),
        compiler_params=_params(("parallel",)),
        name="attn_sample",
    )(proj, idxp, idxp, kv_all, ik_all)


def _gdn_pre_kernel(qc_ref, kc_ref, vc_ref, qp_ref, kp_ref, vp_ref, misc_ref, conv0_ref, convw_ref,
                    alog_ref, dtb_ref, u_ref, wq_ref, qk_ref, ktt_ref, egl_ref, ctail_ref, ext_scr, *,
                    c_len, cb, steps):
    i = pl.program_id(1)
    rows = cb * c_len

    @pl.when(i == 0)
    def _():
        ext_scr[0:8, :] = conv0_ref[0]

    @pl.when(i > 0)
    def _():
        ext_scr[0:8, 0:B_QK] = qp_ref[...]
        ext_scr[0:8, B_QK:2 * B_QK] = kp_ref[...]
        ext_scr[0:8, 2 * B_QK:B_CONV_CH] = vp_ref[...]

    ext_scr[8:8 + rows, 0:B_QK] = qc_ref[...]
    ext_scr[8:8 + rows, B_QK:2 * B_QK] = kc_ref[...]
    ext_scr[8:8 + rows, 2 * B_QK:B_CONV_CH] = vc_ref[...]
    w = convw_ref[...]
    y = None
    for j in range(CONV_WIDTH):
        off = 8 - (CONV_WIDTH - 1) + j
        term = ext_scr[off:off + rows, :] * w[j:j + 1, :]
        y = term if y is None else y + term
    y = _silu(y)

    @pl.when(i == steps - 1)
    def _():
        ctail_ref[0] = ext_scr[rows:rows + 8, :]

    misc = misc_ref[...]
    sp_in = misc + dtb_ref[...]
    softplus = jnp.maximum(sp_in, 0.0) + jnp.log1p(jnp.exp(-jnp.abs(sp_in)))
    g_tile = -jnp.exp(alog_ref[...]) * softplus
    beta_tile = _sigmoid(misc)

    rr = lax.broadcasted_iota(I32, (rows, rows), 0)
    cc = lax.broadcasted_iota(I32, (rows, rows), 1)
    c_shift = c_len.bit_length() - 1
    assert (1 << c_shift) == c_len
    same = (rr >> c_shift) == (cc >> c_shift)
    tril = jnp.where(same, jnp.where(rr >= cc, 1.0, 0.0), 0.0).astype(BF16)
    g_cum = None
    for piece in _split3(g_tile):
        part = _dot(tril, piece)
        g_cum = part if g_cum is None else g_cum + part
    e_cum = jnp.exp(g_cum)

    row = lax.broadcasted_iota(I32, (c_len, c_len), 0)
    col = lax.broadcasted_iota(I32, (c_len, c_len), 1)
    incl = row >= col
    strict = row > col
    lane_c = lax.broadcasted_iota(I32, (c_len, 128), 1)
    eye_k = _eye(B_KEY_DIM, B_KEY_DIM, BF16)

    chains = []
    for ci in range(cb):
        r0 = ci * c_len
        gcum_c = g_cum[r0:r0 + c_len, :]
        ecum_c = e_cum[r0:r0 + c_len, :]
        g_last = gcum_c[c_len - 1:c_len, :]
        tail_c = jnp.exp(g_last - gcum_c)
        egl_ref[ci] = jnp.broadcast_to(ecum_c[c_len - 1:c_len, :], (8, 128))
        gc_pieces = _split3(gcum_c)
        for h in range(B_HEADS):
            qh = y[r0:r0 + c_len, h * B_KEY_DIM:(h + 1) * B_KEY_DIM]
            kh = y[r0:r0 + c_len, B_QK + h * B_KEY_DIM:B_QK + (h + 1) * B_KEY_DIM]
            vh = y[r0:r0 + c_len, 2 * B_QK + h * B_VAL_DIM:2 * B_QK + (h + 1) * B_VAL_DIM]
            qh = qh * lax.rsqrt(jnp.sum(qh * qh, axis=-1, keepdims=True) + RMS_EPS)
            kh = kh * lax.rsqrt(jnp.sum(kh * kh, axis=-1, keepdims=True) + RMS_EPS)
            lane = MISC_A + h
            beta = beta_tile[r0:r0 + c_len, MISC_B + h:MISC_B + h + 1]
            gc = gcum_c[:, lane:lane + 1]
            eg = ecum_c[:, lane:lane + 1]
            pick = (lane_c == lane).astype(BF16)
            gr = None
            for piece in gc_pieces:
                part = _dot_nt(pick, piece)
                gr = part if gr is None else gr + part
            decay = jnp.where(incl, jnp.exp(jnp.where(incl, gc - gr, 0.0)), 0.0)
            kb = kh * beta
            m = jnp.where(strict, _mm1_nt(kb, kh) * decay, 0.0)
            rhs = jnp.concatenate([vh * beta, kb * eg], axis=1)
            qs = qh * (B_KEY_DIM ** -0.5)
            qk = jnp.where(incl, _mm1_nt(qs, kh) * decay, 0.0)
            kt = (kh * tail_c[:, lane:lane + 1]).astype(BF16)
            hs = slice(h * B_VAL_DIM, (h + 1) * B_VAL_DIM)
            wq_ref[ci, c_len:2 * c_len, hs] = (qs * eg).astype(BF16)
            qk_ref[ci, h] = qk.astype(BF16)
            ktt_ref[ci, h] = _dot_nt(eye_k, kt).astype(BF16)
            chains.append((ci, r0, hs, m, rhs))

    eye_c = _eye(c_len, c_len, F32)
    xs = [-ch[3] for ch in chains]
    ps = [eye_c + x for x in xs]
    n = 2
    while n < c_len:
        xs = [_mm3(x, x) for x in xs]
        ps = [p + _mm3(p, x) for p, x in zip(ps, xs)]
        n *= 2
    sols = [_mm3(p, ch[4]) for p, ch in zip(ps, chains)]
    for (ci, r0, hs, _, _), sol in zip(chains, sols):
        u_ref[r0:r0 + c_len, hs] = sol[:, 0:B_VAL_DIM]
        wq_ref[ci, 0:c_len, hs] = sol[:, B_VAL_DIM:].astype(BF16)


def _gdn_state_kernel(*refs, c_len, n_chunks, gb):
    u_ref, wq_ref, qk_ref, ktt_ref, egl_ref = refs[0:5]
    z_refs = refs[5:5 + gb]
    s0_ref, gn_ref, o_ref, s_out_ref, s_scr = refs[5 + gb:]
    c = pl.program_id(1)

    @pl.when(c == 0)
    def _():
        s_scr[...] = s0_ref[...]

    ids = [(g, h, slice(h * B_VAL_DIM, (h + 1) * B_VAL_DIM)) for g in range(gb) for h in range(B_HEADS)]
    a1s = [_dot(wq_ref[g, :, hs], s_scr[g, h].astype(BF16)) for g, h, hs in ids]
    vbs = [(u_ref[g, :, hs] - a1[0:c_len]).astype(BF16) for (g, h, hs), a1 in zip(ids, a1s)]
    upd = [_dot(ktt_ref[g, 0, h], vb) for (g, h, hs), vb in zip(ids, vbs)]
    for (g, h, hs), d in zip(ids, upd):
        s_scr[g, h] = s_scr[g, h] * egl_ref[g, 0, 0:1, MISC_A + h:MISC_A + h + 1] + d
    outs = [a1[c_len:2 * c_len] + _dot(qk_ref[g, 0, h], vb) for (g, h, hs), a1, vb in zip(ids, a1s, vbs)]
    for (g, h, hs), o in zip(ids, outs):
        o = o * lax.rsqrt(jnp.mean(o * o, axis=-1, keepdims=True) + RMS_EPS) * gn_ref[...]
        o = o * _silu(z_refs[g][:, hs])
        o_ref[g, :, hs] = o.astype(o_ref.dtype)

    @pl.when(c == n_chunks - 1)
    def _():
        s_out_ref[...] = s_scr[...]


def _gated_deltanet(proj, conv0_pad, s0, conv_w, alog_row, dtb_row, gnorm_row, bsz, tlen, row0):
    c_len = CHUNK if tlen % CHUNK == 0 else tlen
    n_chunks = tlen // c_len
    cb = 4 if n_chunks % 4 == 0 else 1
    rows = cb * c_len
    steps = n_chunks // cb
    assert row0 % rows == 0 and row0 % 8 == 0 and c_len >= 8
    rb0 = row0 // rows

    def cur(col_block):
        return pl.BlockSpec((rows, B_QK), lambda b, i: (rb0 + b * steps + i, col_block))

    def prev(col_block):
        return pl.BlockSpec((8, B_QK),
                            lambda b, i: (jnp.maximum((row0 + b * tlen + i * rows) // 8 - 1, 0), col_block))

    const2 = lambda shape: pl.BlockSpec(shape, lambda b, i: (0, 0))
    nck = bsz * n_chunks
    u, wq, qk, ktt, egl, ctail = pl.pallas_call(
        functools.partial(_gdn_pre_kernel, c_len=c_len, cb=cb, steps=steps),
        grid=(bsz, steps),
        in_specs=[cur(COL_QB // B_QK), cur(COL_KB // B_QK), cur(COL_VB // B_QK),
                  prev(COL_QB // B_QK), prev(COL_KB // B_QK), prev(COL_VB // B_QK),
                  pl.BlockSpec((rows, 128), lambda b, i: (rb0 + b * steps + i, COL_MISC // 128)),
                  pl.BlockSpec((1, 8, B_CONV_CH), lambda b, i: (b, 0, 0)),
                  const2((CONV_WIDTH, B_CONV_CH)), const2((1, 128)), const2((1, 128))],
        out_specs=[pl.BlockSpec((rows, B_V), lambda b, i: (b * steps + i, 0)),
                   pl.BlockSpec((cb, 2 * c_len, B_V), lambda b, i: (b * steps + i, 0, 0)),
                   pl.BlockSpec((cb, B_HEADS, c_len, c_len), lambda b, i: (b * steps + i, 0, 0, 0)),
                   pl.BlockSpec((cb, B_HEADS, B_KEY_DIM, c_len), lambda b, i: (b * steps + i, 0, 0, 0)),
                   pl.BlockSpec((cb, 8, 128), lambda b, i: (b * steps + i, 0, 0)),
                   pl.BlockSpec((1, 8, B_CONV_CH), lambda b, i: (b, 0, 0))],
        out_shape=[jax.ShapeDtypeStruct((bsz * tlen, B_V), F32),
                   jax.ShapeDtypeStruct((nck, 2 * c_len, B_V), BF16),
                   jax.ShapeDtypeStruct((nck, B_HEADS, c_len, c_len), BF16),
                   jax.ShapeDtypeStruct((nck, B_HEADS, B_KEY_DIM, c_len), BF16),
                   jax.ShapeDtypeStruct((nck, 8, 128), F32),
                   jax.ShapeDtypeStruct((bsz, 8, B_CONV_CH), F32)],
        scratch_shapes=[pltpu.VMEM((8 + rows, B_CONV_CH), F32)],
        compiler_params=_params(("parallel", "arbitrary")),
        name="gdn_pre",
    )(proj, proj, proj, proj, proj, proj, proj, conv0_pad, conv_w, alog_row, dtb_row)

    gb = 4 if bsz % 4 == 0 else 1
    zrb0 = row0 // c_len

    def z_spec(g):
        return pl.BlockSpec((c_len, B_V),
                            lambda bg, c, g=g: (zrb0 + (bg * gb + g) * n_chunks + c, COL_Z // B_V))

    o_b, s_new = pl.pallas_call(
        functools.partial(_gdn_state_kernel, c_len=c_len, n_chunks=n_chunks, gb=gb),
        grid=(bsz // gb, n_chunks),
        in_specs=[pl.BlockSpec((gb, c_len, B_V), lambda bg, c: (bg, c, 0)),
                  pl.BlockSpec((gb, 2 * c_len, B_V), lambda bg, c: (bg, c, 0)),
                  pl.BlockSpec((gb, 1, B_HEADS, c_len, c_len), lambda bg, c: (bg, c, 0, 0, 0)),
                  pl.BlockSpec((gb, 1, B_HEADS, B_KEY_DIM, c_len), lambda bg, c: (bg, c, 0, 0, 0)),
                  pl.BlockSpec((gb, 1, 8, 128), lambda bg, c: (bg, c, 0, 0))]
                 + [z_spec(g) for g in range(gb)]
                 + [pl.BlockSpec((gb, B_HEADS, B_KEY_DIM, B_VAL_DIM), lambda bg, c: (bg, 0, 0, 0)),
                    pl.BlockSpec((1, B_VAL_DIM), lambda bg, c: (0, 0))],
        out_specs=[pl.BlockSpec((gb, c_len, B_V), lambda bg, c: (bg, c, 0)),
                   pl.BlockSpec((gb, B_HEADS, B_KEY_DIM, B_VAL_DIM), lambda bg, c: (bg, 0, 0, 0))],
        out_shape=[jax.ShapeDtypeStruct((bsz, tlen, B_V), BF16),
                   jax.ShapeDtypeStruct((bsz, B_HEADS, B_KEY_DIM, B_VAL_DIM), F32)],
        scratch_shapes=[pltpu.VMEM((gb, B_HEADS, B_KEY_DIM, B_VAL_DIM), F32)],
        compiler_params=_params(("parallel", "arbitrary")),
        name="gdn_state",
    )(u.reshape(bsz, tlen, B_V), wq.reshape(bsz, n_chunks * 2 * c_len, B_V),
      qk.reshape(bsz, n_chunks, B_HEADS, c_len, c_len), ktt.reshape(bsz, n_chunks, B_HEADS, B_KEY_DIM, c_len),
      egl.reshape(bsz, n_chunks, 8, 128), *([proj] * gb), s0, gnorm_row)
    return o_b.reshape(bsz * tlen, B_V), s_new, ctail[:, 8 - (CONV_WIDTH - 1):, :]


def _post_kernel(*refs, aliased):
    oa_ref, ob_ref, ga_ref, gb_ref, h_ref, woa_ref, wob_ref, wout_ref, g_ref, b_ref = refs[0:10]
    o_ref, ob16_ref = refs[-2:]
    ya = _dot(oa_ref[...], woa_ref[...])
    yb = _dot(ob_ref[...], wob_ref[...])
    mix = _sigmoid(ga_ref[...]) * ya + _sigmoid(gb_ref[...]) * yb
    y = _dot(mix.astype(BF16), wout_ref[...])
    out = _ln(DEEPNORM_ALPHA * h_ref[...] + y, g_ref[...], b_ref[...])
    o_ref[...] = out
    ob16_ref[...] = out.astype(BF16)


def _post_mixer_stream(o_a, o_b, proj, h, weights, row0, bufs):
    w_oa, w_ob, w_out, ln_g, ln_b = weights
    n_rows = o_a.shape[0]
    n = h.shape[0]
    tm = _pick_tile(n_rows, (1024, 512, 256, 128, 64, 32, 16))
    assert row0 % tm == 0
    r0 = row0 // tm
    loc = lambda w: pl.BlockSpec((tm, w), lambda i: (i, 0))
    glob = lambda w, cb: pl.BlockSpec((tm, w), lambda i: (r0 + i, cb))
    full = lambda a, b: pl.BlockSpec((a, b), lambda i: (0, 0))
    in_specs = [loc(A_Q), loc(B_V), glob(D_MODEL, COL_GA // D_MODEL), glob(D_MODEL, COL_GB // D_MODEL),
                glob(D_MODEL, 0), full(A_Q, D_MODEL), full(B_V, D_MODEL), full(D_MODEL, D_MODEL),
                full(1, D_MODEL), full(1, D_MODEL)]
    args = [o_a, o_b, proj, proj, h, w_oa, w_ob, w_out, ln_g.reshape(1, D_MODEL), ln_b.reshape(1, D_MODEL)]
    aliases = {}
    if bufs is not None:
        in_specs += [_any_spec(), _any_spec()]
        args += list(bufs)
        aliases = {10: 0, 11: 1}
    return pl.pallas_call(
        functools.partial(_post_kernel, aliased=bufs is not None),
        grid=(n_rows // tm,),
        in_specs=in_specs,
        out_specs=[glob(D_MODEL, 0), glob(D_MODEL, 0)],
        out_shape=[jax.ShapeDtypeStruct((n, D_MODEL), F32), jax.ShapeDtypeStruct((n, D_MODEL), BF16)],
        input_output_aliases=aliases,
        compiler_params=_params(("parallel",)),
        name="post_mixer",
    )(*args)


def _ffn_kernel(h_ref, hb_ref, w1_ref, w3_ref, w2_ref, g_ref, b_ref, o_ref, acc_ref, *, n_ff):
    j = pl.program_id(1)
    hb = hb_ref[...]
    a = _dot(hb, w1_ref[...])
    b = _dot(hb, w3_ref[...])
    part = _dot((_silu(a) * b).astype(BF16), w2_ref[...])

    @pl.when(j == 0)
    def _():
        acc_ref[...] = part

    @pl.when(j > 0)
    def _():
        acc_ref[...] += part

    @pl.when(j == n_ff - 1)
    def _():
        o_ref[...] = _ln(DEEPNORM_ALPHA * h_ref[...] + acc_ref[...], g_ref[...], b_ref[...])


def _dense_ffn(h, hb, w1, w3, w2, ln_g, ln_b):
    n = h.shape[0]
    tm = _pick_tile(n, (768, 512, 256, 128, 64, 32, 16))
    tf = D_FF // 2
    n_ff = D_FF // tf
    return pl.pallas_call(
        functools.partial(_ffn_kernel, n_ff=n_ff),
        grid=(n // tm, n_ff),
        in_specs=[pl.BlockSpec((tm, D_MODEL), lambda i, j: (i, 0)),
                  pl.BlockSpec((tm, D_MODEL), lambda i, j: (i, 0)),
                  pl.BlockSpec((D_MODEL, tf), lambda i, j: (0, j)),
                  pl.BlockSpec((D_MODEL, tf), lambda i, j: (0, j)),
                  pl.BlockSpec((tf, D_MODEL), lambda i, j: (j, 0)),
                  pl.BlockSpec((1, D_MODEL), lambda i, j: (0, 0)),
                  pl.BlockSpec((1, D_MODEL), lambda i, j: (0, 0))],
        out_specs=pl.BlockSpec((tm, D_MODEL), lambda i, j: (i, 0)),
        out_shape=jax.ShapeDtypeStruct((n, D_MODEL), F32),
        scratch_shapes=[pltpu.VMEM((tm, D_MODEL), F32)],
        compiler_params=_params(("parallel", "arbitrary")),
        name="dense_ffn",
    )(h, hb, w1, w3, w2, ln_g.reshape(1, D_MODEL), ln_b.reshape(1, D_MODEL))


def _router_kernel(h_ref, r_ref, gate_ref):
    logits = _mm3(h_ref[...], r_ref[...])
    lane = lax.broadcasted_iota(I32, logits.shape, 1)
    neg_inf = jnp.float32(-jnp.inf)
    lg = jnp.where(lane < N_EXPERTS, logits, neg_inf)
    v1 = jnp.max(lg, axis=-1, keepdims=True)
    i1 = jnp.min(jnp.where(lg == v1, lane, 128), axis=-1, keepdims=True)
    lg2 = jnp.where(lane == i1, neg_inf, lg)
    v2 = jnp.max(lg2, axis=-1, keepdims=True)
    i2 = jnp.min(jnp.where(lg2 == v2, lane, 128), axis=-1, keepdims=True)
    e2 = jnp.exp(v2 - v1)
    w1 = 1.0 / (1.0 + e2)
    w2 = e2 / (1.0 + e2)
    gate_ref[...] = jnp.where(lane == i1, w1, jnp.where(lane == i2, w2, 0.0))


def _router(h, router_pad):
    n = h.shape[0]
    tm = _pick_tile(n, (768, 512, 256, 128, 64, 32, 16, 8))
    return pl.pallas_call(
        _router_kernel,
        grid=(n // tm,),
        in_specs=[pl.BlockSpec((tm, D_MODEL), lambda i: (i, 0)),
                  pl.BlockSpec((D_MODEL, 128), lambda i: (0, 0))],
        out_specs=pl.BlockSpec((tm, 128), lambda i: (i, 0)),
        out_shape=jax.ShapeDtypeStruct((n, 128), F32),
        compiler_params=_params(("parallel",)),
        name="moe_router",
    )(h, router_pad)


def _expert_kernel(te_ref, nt_ref, x_ref, w1_ref, w3_ref, w2_ref, o_ref, acc_ref, *, n_ff):
    i = pl.program_id(0)
    j = pl.program_id(1)

    @pl.when(i < nt_ref[0])
    def _():
        x = x_ref[...]
        a = _dot(x, w1_ref[0])
        b = _dot(x, w3_ref[0])
        part = _dot((_silu(a) * b).astype(BF16), w2_ref[0])

        @pl.when(j == 0)
        def _():
            acc_ref[...] = part

        @pl.when(j > 0)
        def _():
            acc_ref[...] += part

        @pl.when(j == n_ff - 1)
        def _():
            o_ref[...] = acc_ref[...]


def _grouped_experts(x_sorted, tile_expert, n_used, w1, w3, w2, tm):
    n_tiles = x_sorted.shape[0] // tm
    tf = D_FF_EXPERT // 2
    n_ff = D_FF_EXPERT // tf

    def row_map(i, j, te, nt):
        return (jnp.minimum(i, nt[0] - 1), 0)

    def ff_of(i, j, nt):
        return jnp.where(i < nt[0], j, n_ff - 1)

    grid_spec = pltpu.PrefetchScalarGridSpec(
        num_scalar_prefetch=2,
        grid=(n_tiles, n_ff),
        in_specs=[pl.BlockSpec((tm, D_MODEL), row_map),
                  pl.BlockSpec((1, D_MODEL, tf), lambda i, j, te, nt: (te[i], 0, ff_of(i, j, nt))),
                  pl.BlockSpec((1, D_MODEL, tf), lambda i, j, te, nt: (te[i], 0, ff_of(i, j, nt))),
                  pl.BlockSpec((1, tf, D_MODEL), lambda i, j, te, nt: (te[i], ff_of(i, j, nt), 0))],
        out_specs=pl.BlockSpec((tm, D_MODEL), row_map),
        scratch_shapes=[pltpu.VMEM((tm, D_MODEL), F32)],
    )
    return pl.pallas_call(
        functools.partial(_expert_kernel, n_ff=n_ff),
        grid_spec=grid_spec,
        out_shape=jax.ShapeDtypeStruct((n_tiles * tm, D_MODEL), F32),
        compiler_params=_params(("arbitrary", "arbitrary")),
        name="moe_experts",
    )(tile_expert, n_used, x_sorted, w1, w3, w2)


def _combine_kernel(h_ref, y1_ref, y2_ref, g_ref, b_ref, o_ref):
    f = y1_ref[...] + y2_ref[...]
    o_ref[...] = _ln(DEEPNORM_ALPHA * h_ref[...] + f, g_ref[...], b_ref[...])


def _combine_ln(h, y1, y2, ln_g, ln_b):
    n = h.shape[0]
    tm = _pick_tile(n, (768, 512, 256, 128, 64, 32, 16, 8))
    row = pl.BlockSpec((tm, D_MODEL), lambda i: (i, 0))
    vec = pl.BlockSpec((1, D_MODEL), lambda i: (0, 0))
    return pl.pallas_call(
        _combine_kernel,
        grid=(n // tm,),
        in_specs=[row, row, row, vec, vec],
        out_specs=row,
        out_shape=jax.ShapeDtypeStruct((n, D_MODEL), F32),
        compiler_params=_params(("parallel",)),
        name="moe_combine",
    )(h, y1, y2, ln_g.reshape(1, D_MODEL), ln_b.reshape(1, D_MODEL))


def _moe_ffn(h, hb, router, w1, w3, w2, ln_g, ln_b):
    n = h.shape[0]
    router_pad = jnp.zeros((D_MODEL, 128), F32).at[:, :N_EXPERTS].set(router)
    gate = _router(h, router_pad)[:, :N_EXPERTS]
    tm = 512 if n >= 4096 else 64
    top_w, top_e = lax.top_k(gate, 2)
    flat_e = top_e.reshape(-1).astype(I32)
    tok = jnp.arange(2 * n, dtype=I32) // 2
    onehot = (flat_e[:, None] == jnp.arange(N_EXPERTS, dtype=I32)[None, :]).astype(I32)
    csum = jnp.cumsum(onehot, axis=0)
    rank = jnp.sum(onehot * csum, axis=1) - 1
    counts = csum[-1]
    padded = ((counts + tm - 1) // tm) * tm
    starts = jnp.cumsum(counts) - counts
    pend = jnp.cumsum(padded)
    pstarts = pend - padded
    row_of = jnp.sum(onehot * pstarts[None, :], axis=1) + rank
    n_tiles = (2 * n) // tm + N_EXPERTS
    tile_start = jnp.arange(n_tiles, dtype=I32) * tm
    n_used = (pend[-1] // tm).astype(I32).reshape(1)
    last_start = jnp.minimum(tile_start, (n_used[0] - 1) * tm)
    tile_expert = jnp.sum((last_start[:, None] >= pend[None, :]).astype(I32), axis=1).astype(I32)
    _, sorted_tok = lax.sort((flat_e, tok), num_keys=1, is_stable=True)
    r = jnp.arange(n_tiles * tm, dtype=I32)
    e_r = jnp.repeat(tile_expert, tm)
    src_idx = jnp.clip(starts[e_r] + (r - pstarts[e_r]), 0, 2 * n - 1)
    src_tok = sorted_tok[src_idx]
    x_sorted = jnp.take(hb, src_tok, axis=0)
    y_sorted = _grouped_experts(x_sorted, tile_expert, n_used, w1, w3, w2, tm)
    rows = row_of.reshape(n, 2)
    y1 = jnp.take(y_sorted, rows[:, 0], axis=0) * top_w[:, 0:1]
    y2 = jnp.take(y_sorted, rows[:, 1], axis=0) * top_w[:, 1:2]
    return _combine_ln(h, y1, y2, ln_g, ln_b)


def kernel(x_prompt, x_sample, cache_k, cache_v, cache_idx_k, state_conv, state_gdn,
           ln_in_g, ln_in_b, w_in, conv_w, a_log, dt_bias, gnorm_g, w_oa, w_ob, w_out,
           ln1_g, ln1_b, ln2_g, ln2_b, ffn_w1, ffn_w3, ffn_w2,
           moe_router, moe_w1, moe_w3, moe_w2):
    bp, tp, _ = x_prompt.shape
    bs, ts, _ = x_sample.shape
    past = cache_k.shape[2]
    n_p = bp * tp
    n_s = bs * ts
    h = _layer_norm_streams(x_prompt.reshape(n_p, D_MODEL), x_sample.reshape(n_s, D_MODEL), ln_in_g, ln_in_b)

    s_valid = past + ts
    s_pad = ((s_valid + 127) // 128) * 128
    lane_pad = jnp.zeros((1, 128), F32)
    conv0_p = jnp.zeros((bp, 8, B_CONV_CH), F32)
    s0_p = jnp.zeros((bp, B_HEADS, B_KEY_DIM, B_VAL_DIM), F32)

    outs = {k: [] for k in ("kp", "vp", "ikp", "convp", "sp", "ks", "vs", "iks", "convs", "ss")}
    for l in range(DEPTH):
        w_packed, w_idx_hi, w_idx_lo = _pack_w_in(w_in[l])
        proj, k_new, v_new = _project(h, w_packed)
        idxp, ik_new = _project_idx(h, w_idx_hi, w_idx_lo)

        oa_p = _attention_prompt(proj, idxp, bp, tp)
        kv_s = jnp.concatenate([k_new[n_p:], v_new[n_p:]], axis=-1).reshape(bs, ts, 2 * A_KV)
        kv_cache = jnp.concatenate([cache_k[l].reshape(bs, past, A_KV), cache_v[l].reshape(bs, past, A_KV)], axis=-1)
        kv_all = jnp.concatenate([kv_cache, kv_s, jnp.zeros((bs, s_pad - s_valid, 2 * A_KV), F32)], axis=1)
        ik_s = jnp.pad(ik_new[n_p:].reshape(bs, ts, IDX_DIM), ((0, 0), (0, 0), (0, 128 - IDX_DIM)))
        ik_cache = jnp.pad(cache_idx_k[l], ((0, 0), (0, 0), (0, 128 - IDX_DIM)))
        ik_all = jnp.concatenate([ik_cache, ik_s, jnp.zeros((bs, s_pad - s_valid, 128), F32)], axis=1)
        oa_s = _attention_sample(proj, idxp, kv_all, ik_all, bs, ts, n_p, s_valid)

        alog_row = lax.dynamic_update_slice(lane_pad, a_log[l].reshape(1, B_HEADS), (0, MISC_A))
        dtb_row = lax.dynamic_update_slice(lane_pad, dt_bias[l].reshape(1, B_HEADS), (0, MISC_A))
        gn_row = gnorm_g[l].reshape(1, B_VAL_DIM)
        conv0_s = jnp.pad(state_conv[l], ((0, 0), (8 - (CONV_WIDTH - 1), 0), (0, 0)))
        ob_p, s_p, qkv_p = _gated_deltanet(proj, conv0_p, s0_p, conv_w[l], alog_row, dtb_row, gn_row, bp, tp, 0)
        ob_s, s_s, qkv_s = _gated_deltanet(proj, conv0_s, state_gdn[l], conv_w[l], alog_row, dtb_row, gn_row,
                                           bs, ts, n_p)

        weights = (w_oa[l].astype(BF16), w_ob[l].astype(BF16), w_out[l].astype(BF16), ln1_g[l], ln1_b[l])
        bufs = _post_mixer_stream(oa_p, ob_p, proj, h, weights, 0, None)
        h1, h1b = _post_mixer_stream(oa_s, ob_s, proj, h, weights, n_p, bufs)
        i = l // 2
        if l % 2 == 0:
            h = _dense_ffn(h1, h1b, ffn_w1[i].astype(BF16), ffn_w3[i].astype(BF16), ffn_w2[i].astype(BF16),
                           ln2_g[l], ln2_b[l])
        else:
            h = _moe_ffn(h1, h1b, moe_router[i], moe_w1[i].astype(BF16), moe_w3[i].astype(BF16),
                         moe_w2[i].astype(BF16), ln2_g[l], ln2_b[l])

        outs["kp"].append(k_new[:n_p].reshape(bp, tp, A_KV_HEADS, A_HEAD_DIM))
        outs["vp"].append(v_new[:n_p].reshape(bp, tp, A_KV_HEADS, A_HEAD_DIM))
        outs["ikp"].append(ik_new[:n_p].reshape(bp, tp, IDX_DIM))
        outs["convp"].append(qkv_p)
        outs["sp"].append(s_p)
        outs["ks"].append(k_new[n_p:].reshape(bs, ts, A_KV_HEADS, A_HEAD_DIM))
        outs["vs"].append(v_new[n_p:].reshape(bs, ts, A_KV_HEADS, A_HEAD_DIM))
        outs["iks"].append(ik_new[n_p:].reshape(bs, ts, IDX_DIM))
        outs["convs"].append(qkv_s)
        outs["ss"].append(s_s)

    st = lambda name: jnp.stack(outs[name])
    return (h[:n_p].reshape(bp, tp, D_MODEL), h[n_p:].reshape(bs, ts, D_MODEL),
            st("kp"), st("vp"), st("ikp"), st("convp"), st("sp"),
            st("ks"), st("vs"), st("iks"), st("convs"), st("ss"))
```
